```python
import math
import jax
import jax.numpy as jnp
from jax import lax
import numpy as np

D_MODEL = 1024
BATCH = 2
SEQ = 8192
DEPTH = 2

N_ATTN_LAYERS = (DEPTH + 1) // 2
N_SSD_LAYERS = DEPTH // 2

HEAD_DIM = 64
DIFF_HEADS = 4
DIFF_V = 2 * HEAD_DIM
SB_HEADS = 8
SB_DIM = HEAD_DIM
DIFF_QK_W = DIFF_HEADS * 2 * HEAD_DIM
SB_W = SB_HEADS * SB_DIM
MIX_WIDTH = DIFF_HEADS * DIFF_V + SB_W
ATTN_IN_WIDTH = 3 * DIFF_QK_W + 3 * SB_W
Q_BLOCK = 128
ROPE_THETA = 10000.0
NORM_EPS = 1e-6
SUBLN_EPS = 1e-5

SSD_EXPAND = 2
D_INNER = SSD_EXPAND * D_MODEL
SSD_HEAD_DIM = 64
SSD_HEADS = D_INNER // SSD_HEAD_DIM
SSD_GROUPS = 8
SSD_STATE = 128
SSD_CONV = 4
SSD_CHUNK = 128
SSD_GN = SSD_GROUPS * SSD_STATE
CONV_DIM = D_INNER + 2 * SSD_GN
SSD_IN_WIDTH = 2 * D_INNER + 2 * SSD_GN + SSD_HEADS

D_FF = (((8 * D_MODEL + 2) // 3 + 255) // 256) * 256

kernel_name = "hybrid_diffattn_stickbreak_ssd_swiglu"


def rmsnorm(x, w, eps=NORM_EPS):
    xf = x.astype(jnp.float32)
    y = xf * lax.rsqrt(jnp.mean(xf * xf, axis=-1, keepdims=True) + eps)
    return (y * w.astype(jnp.float32)).astype(x.dtype)


def rope(t, pos):
    half = HEAD_DIM // 2
    inv_freq = ROPE_THETA ** (-jnp.arange(half, dtype=jnp.float32) / half)
    ang = pos.astype(jnp.float32)[:, None] * inv_freq[None, :]
    shape = (1, pos.shape[0]) + (1,) * (t.ndim - 3) + (half,)
    cos = jnp.cos(ang).reshape(shape)
    sin = jnp.sin(ang).reshape(shape)
    tf = t.astype(jnp.float32)
    t1, t2 = tf[..., :half], tf[..., half:]
    out = jnp.concatenate([t1 * cos - t2 * sin, t2 * cos + t1 * sin], axis=-1)
    return out.astype(t.dtype)


def swiglu(h, w_gate, w_up, w_down):
    return (jax.nn.silu(h @ w_gate) * (h @ w_up)) @ w_down


def attn_mixer(h, w_in, lq1, lk1, lq2, lk2, subln_w, w_out, lambda_init):
    f32 = jnp.float32
    b, s, _ = h.shape
    nb = s // Q_BLOCK
    pos = jnp.arange(s, dtype=jnp.int32)
    proj = h @ w_in
    dq, dk, dv, sq, sk, sv = jnp.split(
        proj,
        [DIFF_QK_W, 2 * DIFF_QK_W, 3 * DIFF_QK_W, 3 * DIFF_QK_W + SB_W, 3 * DIFF_QK_W + 2 * SB_W],
        axis=-1)
    dq = rope(dq.reshape(b, s, DIFF_HEADS, 2, HEAD_DIM), pos).transpose(0, 2, 3, 1, 4)
    dk = rope(dk.reshape(b, s, DIFF_HEADS, 2, HEAD_DIM), pos).transpose(0, 2, 3, 1, 4)
    dv = dv.reshape(b, s, DIFF_HEADS, DIFF_V).transpose(0, 2, 1, 3)
    sq = sq.reshape(b, s, SB_HEADS, SB_DIM).transpose(0, 2, 1, 3)
    sk = sk.reshape(b, s, SB_HEADS, SB_DIM).transpose(0, 2, 1, 3)
    sv = sv.reshape(b, s, SB_HEADS, SB_DIM).transpose(0, 2, 1, 3)
    scale = HEAD_DIM ** -0.5
    lam = (jnp.exp(jnp.sum(lq1.astype(f32) * lk1.astype(f32)))
           - jnp.exp(jnp.sum(lq2.astype(f32) * lk2.astype(f32))) + lambda_init)

    dq_blocks = jnp.moveaxis(dq.reshape(b, DIFF_HEADS, 2, nb, Q_BLOCK, HEAD_DIM), 3, 0)
    sq_blocks = jnp.moveaxis(sq.reshape(b, SB_HEADS, nb, Q_BLOCK, SB_DIM), 2, 0)
    starts = jnp.arange(nb, dtype=jnp.int32) * Q_BLOCK

    def block(args):
        start, dq_b, sq_b = args
        qpos = start + jnp.arange(Q_BLOCK, dtype=jnp.int32)
        causal = pos[None, :] <= qpos[:, None]
        strict = pos[None, :] < qpos[:, None]
        sc = jnp.einsum('bhiqd,bhikd->bhiqk', dq_b, dk).astype(f32) * scale
        p = jax.nn.softmax(jnp.where(causal, sc, -jnp.inf), axis=-1)
        p_diff = p[:, :, 0] - lam * p[:, :, 1]
        o_diff = jnp.einsum('bhqk,bhkv->bhqv', p_diff.astype(dv.dtype), dv)
        z = jnp.einsum('bhqd,bhkd->bhqk', sq_b, sk).astype(f32) * scale
        log_keep = jnp.where(strict, jax.nn.log_sigmoid(-z), 0.0)
        tail = lax.cumsum(log_keep, axis=3, reverse=True) - log_keep
        a_sb = jnp.where(strict, jnp.exp(jax.nn.log_sigmoid(z) + tail), 0.0)
        o_sb = jnp.einsum('bhqk,bhkd->bhqd', a_sb.astype(sv.dtype), sv)
        return o_diff, o_sb

    o_diff, o_sb = lax.map(block, (starts, dq_blocks, sq_blocks))
    o_diff = o_diff.transpose(1, 0, 3, 2, 4).reshape(b, s, DIFF_HEADS, DIFF_V)
    o_diff = rmsnorm(o_diff, subln_w, SUBLN_EPS) * (1.0 - lambda_init)
    o_sb = o_sb.transpose(1, 0, 3, 2, 4).reshape(b, s, SB_W)
    mixed = jnp.concatenate([o_diff.reshape(b, s, DIFF_HEADS * DIFF_V), o_sb], axis=-1)
    return mixed @ w_out


def causal_depthwise_conv(u, w, bias):
    out = lax.conv_general_dilated(
        u, w[:, None, :].astype(u.dtype), window_strides=(1,), padding=[(SSD_CONV - 1, 0)],
        dimension_numbers=('NWC', 'WIO', 'NWC'), feature_group_count=u.shape[-1])
    return out + bias


def ssd_scan(x, dt, a, bmat, cmat):
    f32 = jnp.float32
    b, s, nh, p = x.shape
    nc = s // SSD_CHUNK
    r = nh // SSD_GROUPS
    L = SSD_CHUNK
    xdt = (x.astype(f32) * dt[..., None]).reshape(b, nc, L, SSD_GROUPS, r, p)
    acum = jnp.cumsum((dt * a).reshape(b, nc, L, SSD_GROUPS, r), axis=2)
    acum = acum.transpose(0, 1, 3, 4, 2)
    bm = bmat.astype(f32).reshape(b, nc, L, SSD_GROUPS, SSD_STATE)
    cm = cmat.astype(f32).reshape(b, nc, L, SSD_GROUPS, SSD_STATE)
    idx = jnp.arange(L)
    lower = idx[:, None] >= idx[None, :]
    seg = acum[..., :, None] - acum[..., None, :]
    decay = jnp.exp(jnp.where(lower, seg, -jnp.inf))
    cb = jnp.einsum('bclgn,bcmgn->bcglm', cm, bm)
    y_diag = jnp.einsum('bcgrlm,bcmgrp->bclgrp', cb[:, :, :, None] * decay, xdt)
    state_decay = jnp.exp(acum[..., -1:] - acum)
    states = jnp.einsum('bclgn,bcgrl,bclgrp->bcgrpn', bm, state_decay, xdt)
    chunk_decay = jnp.exp(acum[..., -1])

    def step(carry, inp):
        st, dec = inp
        return carry * dec[..., None, None] + st, carry

    init = jnp.zeros((b, SSD_GROUPS, r, p, SSD_STATE), f32)
    _, prev = lax.scan(step, init, (jnp.moveaxis(states, 1, 0), jnp.moveaxis(chunk_decay, 1, 0)))
    prev = jnp.moveaxis(prev, 0, 1)
    y_off = jnp.einsum('bclgn,bcgrpn,bcgrl->bclgrp', cm, prev, jnp.exp(acum))
    return (y_diag + y_off).reshape(b, s, nh, p)


def ssd_mixer(h, w_in, conv_w, conv_b, dt_bias, a_log, d_skip, gnorm_w, w_out):
    f32 = jnp.float32
    b, s, _ = h.shape
    proj = h @ w_in
    z, xbc, dt = jnp.split(proj, [D_INNER, D_INNER + CONV_DIM], axis=-1)
    xbc = jax.nn.silu(causal_depthwise_conv(xbc, conv_w, conv_b))
    xs, bmat, cmat = jnp.split(xbc, [D_INNER, D_INNER + SSD_GN], axis=-1)
    xs = xs.reshape(b, s, SSD_HEADS, SSD_HEAD_DIM)
    dt = jax.nn.softplus(dt.astype(f32) + dt_bias.astype(f32))
    a = -jnp.exp(a_log.astype(f32))
    y = ssd_scan(xs, dt, a,
                 bmat.reshape(b, s, SSD_GROUPS, SSD_STATE),
                 cmat.reshape(b, s, SSD_GROUPS, SSD_STATE))
    y = y + d_skip.astype(f32)[:, None] * xs.astype(f32)
    gated = y.reshape(b, s, D_INNER) * jax.nn.silu(z.astype(f32))
    gated = gated.reshape(b, s, SSD_GROUPS, D_INNER // SSD_GROUPS)
    gated = gated * lax.rsqrt(jnp.mean(gated * gated, axis=-1, keepdims=True) + SUBLN_EPS)
    y = (gated.reshape(b, s, D_INNER) * gnorm_w.astype(f32)).astype(h.dtype)
    return y @ w_out


def setup_inputs(seed: int = 0) -> dict:
    key = jax.random.key(seed)
    ks = iter(jax.random.split(key, 32))

    def nrm(shape, scale):
        return jax.random.normal(next(ks), shape, jnp.float32) * scale

    def gain(shape):
        return 1.0 + nrm(shape, 0.02)

    na, ns = N_ATTN_LAYERS, N_SSD_LAYERS
    x = nrm((BATCH, SEQ, D_MODEL), 1.0)
    attn_norm = gain((na, D_MODEL))
    attn_w_in = nrm((na, D_MODEL, ATTN_IN_WIDTH), D_MODEL ** -0.5)
    diff_lq1 = nrm((na, HEAD_DIM), 0.1)
    diff_lk1 = nrm((na, HEAD_DIM), 0.1)
    diff_lq2 = nrm((na, HEAD_DIM), 0.1)
    diff_lk2 = nrm((na, HEAD_DIM), 0.1)
    diff_subln = gain((na, DIFF_V))
    attn_w_out = nrm((na, MIX_WIDTH, D_MODEL), MIX_WIDTH ** -0.5)
    ssd_norm = gain((ns, D_MODEL))
    ssd_w_in = nrm((ns, D_MODEL, SSD_IN_WIDTH), D_MODEL ** -0.5)
    ssd_conv_w = nrm((ns, SSD_CONV, CONV_DIM), SSD_CONV ** -0.5)
    ssd_conv_b = nrm((ns, CONV_DIM), 0.02)
    u = jax.random.uniform(next(ks), (ns, SSD_HEADS), jnp.float32)
    dt0 = jnp.exp(u * (math.log(0.1) - math.log(0.001)) + math.log(0.001))
    ssd_dt_bias = dt0 + jnp.log(-jnp.expm1(-dt0))
    ssd_a_log = jnp.log(jax.random.uniform(next(ks), (ns, SSD_HEADS), jnp.float32, 1.0, 16.0))
    ssd_d = gain((ns, SSD_HEADS))
    ssd_gnorm = gain((ns, D_INNER))
    ssd_w_out = nrm((ns, D_INNER, D_MODEL), D_INNER ** -0.5)
    ffn_norm = gain((DEPTH, D_MODEL))
    ffn_w_gate = nrm((DEPTH, D_MODEL, D_FF), D_MODEL ** -0.5)
    ffn_w_up = nrm((DEPTH, D_MODEL, D_FF), D_MODEL ** -0.5)
    ffn_w_down = nrm((DEPTH, D_FF, D_MODEL), D_FF ** -0.5)
    final_norm = gain((D_MODEL,))
    return {
        "x": x,
        "attn_norm": attn_norm, "attn_w_in": attn_w_in,
        "diff_lq1": diff_lq1, "diff_lk1": diff_lk1, "diff_lq2": diff_lq2, "diff_lk2": diff_lk2,
        "diff_subln": diff_subln, "attn_w_out": attn_w_out,
        "ssd_norm": ssd_norm, "ssd_w_in": ssd_w_in, "ssd_conv_w": ssd_conv_w, "ssd_conv_b": ssd_conv_b,
        "ssd_dt_bias": ssd_dt_bias, "ssd_a_log": ssd_a_log, "ssd_d": ssd_d, "ssd_gnorm": ssd_gnorm,
        "ssd_w_out": ssd_w_out,
        "ffn_norm": ffn_norm, "ffn_w_gate": ffn_w_gate, "ffn_w_up": ffn_w_up, "ffn_w_down": ffn_w_down,
        "final_norm": final_norm,
    }


def reference(x, attn_norm, attn_w_in, diff_lq1, diff_lk1, diff_lq2, diff_lk2, diff_subln, attn_w_out,
              ssd_norm, ssd_w_in, ssd_conv_w, ssd_conv_b, ssd_dt_bias, ssd_a_log, ssd_d, ssd_gnorm,
              ssd_w_out, ffn_norm, ffn_w_gate, ffn_w_up, ffn_w_down, final_norm):
    h = x
    for layer in range(DEPTH):
        i = layer // 2
        if layer % 2 == 0:
            lambda_init = 0.8 - 0.6 * math.exp(-0.3 * layer)
            h = h + attn_mixer(rmsnorm(h, attn_norm[i]), attn_w_in[i],
                               diff_lq1[i], diff_lk1[i], diff_lq2[i], diff_lk2[i],
                               diff_subln[i], attn_w_out[i], lambda_init)
        else:
            h = h + ssd_mixer(rmsnorm(h, ssd_norm[i]), ssd_w_in[i], ssd_conv_w[i], ssd_conv_b[i],
                              ssd_dt_bias[i], ssd_a_log[i], ssd_d[i], ssd_gnorm[i], ssd_w_out[i])
        h = h + swiglu(rmsnorm(h, ffn_norm[layer]), ffn_w_gate[layer], ffn_w_up[layer], ffn_w_down[layer])
    return rmsnorm(h, final_norm)
```

```python
import functools
import math

import jax
import jax.numpy as jnp
from jax import lax
from jax.experimental import pallas as pl
from jax.experimental.pallas import tpu as pltpu

F32 = jnp.float32
BF16 = jnp.bfloat16

HEAD_DIM = 64
DIFF_HEADS = 4
SB_HEADS = 8
ROPE_THETA = 10000.0
NORM_EPS = 1e-6
SUBLN_EPS = 1e-5
SSD_HEADS = 32
SSD_GROUPS = 8
SSD_STATE = 128
SSD_CONV = 4
SSD_HEAD_DIM = 64
LANES = 128
SUBLANES = 8
VMEM_LIMIT = 56 * 1024 * 1024

F32_EXP_ZERO = -104.0


def _rms(x, g, eps):
    return x * lax.rsqrt(jnp.mean(x * x, axis=-1, keepdims=True) + eps) * g


def _sigmoid(x):
    return 1.0 / (1.0 + jnp.exp(-x))


def _softplus(x):
    return jnp.maximum(x, 0.0) + jnp.log1p(jnp.exp(-jnp.abs(x)))


def _split2(x):
    hi = x.astype(BF16)
    lo = (x - hi.astype(F32)).astype(BF16)
    return hi, lo


def _split3(x):
    hi = x.astype(BF16)
    r = x - hi.astype(F32)
    mid = r.astype(BF16)
    lo = (r - mid.astype(F32)).astype(BF16)
    return hi, mid, lo


def _dot(a, b):
    return jnp.dot(a, b, preferred_element_type=F32)


def _dot_nt(a, b):
    return lax.dot_general(a, b, (((1,), (1,)), ((), ())), preferred_element_type=F32)


def _attn_proj_kernel(x_ref, g_ref, w_ref, cos_ref, sin_ref, o_ref, xn_ref):
    j = pl.program_id(1)

    @pl.when(j == 0)
    def _():
        xn_ref[...] = _rms(x_ref[...], g_ref[...], NORM_EPS).astype(BF16)

    acc = _dot(xn_ref[...], w_ref[...])
    tm, tn = acc.shape

    @pl.when(j == 0)
    def _():
        cos = cos_ref[...]
        sin = sin_ref[...]
        lane = lax.broadcasted_iota(jnp.int32, (tm, LANES), 1)
        first_half = (lane & (HEAD_DIM // 2)) == 0
        for c in range(tn // LANES):
            t = acc[:, c * LANES:(c + 1) * LANES]
            partner = jnp.where(first_half,
                                pltpu.roll(t, LANES - HEAD_DIM // 2, 1),
                                pltpu.roll(t, HEAD_DIM // 2, 1))
            o_ref[:, c * LANES:(c + 1) * LANES] = (t * cos + partner * sin).astype(o_ref.dtype)

    @pl.when(j != 0)
    def _():
        o_ref[...] = acc.astype(o_ref.dtype)


def _attn_proj(x2d, g, w_bf, cos, sin, seq, tm):
    m, d = x2d.shape
    n = w_bf.shape[1]
    tn = 2 * DIFF_HEADS * 2 * HEAD_DIM
    pos_blocks = seq // tm
    return pl.pallas_call(
        _attn_proj_kernel,
        grid=(m // tm, n // tn),
        in_specs=[
            pl.BlockSpec((tm, d), lambda i, j: (i, 0)),
            pl.BlockSpec((1, d), lambda i, j: (0, 0)),
            pl.BlockSpec((d, tn), lambda i, j: (0, j)),
            pl.BlockSpec((tm, LANES), lambda i, j: (i % pos_blocks, 0)),
            pl.BlockSpec((tm, LANES), lambda i, j: (i % pos_blocks, 0)),
        ],
        out_specs=pl.BlockSpec((tm, tn), lambda i, j: (i, j)),
        out_shape=jax.ShapeDtypeStruct((m, n), BF16),
        scratch_shapes=[pltpu.VMEM((tm, d), BF16)],
        compiler_params=pltpu.CompilerParams(
            dimension_semantics=("arbitrary", "arbitrary"), vmem_limit_bytes=VMEM_LIMIT),
        name="attn_proj",
    )(x2d, g, w_bf, cos, sin)


def _ssd_proj_kernel(x_ref, g_ref, w_ref, wdt_ref, o_ref, dt_ref, xn_ref):
    j = pl.program_id(1)

    @pl.when(j == 0)
    def _():
        xn = _rms(x_ref[...], g_ref[...], NORM_EPS).astype(BF16)
        xn_ref[...] = xn
        dt_ref[...] = _dot(xn, wdt_ref[...])

    o_ref[...] = _dot(xn_ref[...], w_ref[...]).astype(o_ref.dtype)


def _ssd_proj(x2d, g, w_bf, wdt_bf, tm, tn):
    m, d = x2d.shape
    n = w_bf.shape[1]
    return pl.pallas_call(
        _ssd_proj_kernel,
        grid=(m // tm, n // tn),
        in_specs=[
            pl.BlockSpec((tm, d), lambda i, j: (i, 0)),
            pl.BlockSpec((1, d), lambda i, j: (0, 0)),
            pl.BlockSpec((d, tn), lambda i, j: (0, j)),
            pl.BlockSpec((d, LANES), lambda i, j: (0, 0)),
        ],
        out_specs=[
            pl.BlockSpec((tm, tn), lambda i, j: (i, j)),
            pl.BlockSpec((tm, LANES), lambda i, j: (i, 0)),
        ],
        out_shape=[jax.ShapeDtypeStruct((m, n), BF16), jax.ShapeDtypeStruct((m, LANES), F32)],
        scratch_shapes=[pltpu.VMEM((tm, d), BF16)],
        compiler_params=pltpu.CompilerParams(
            dimension_semantics=("arbitrary", "arbitrary"), vmem_limit_bytes=VMEM_LIMIT),
        name="ssd_proj",
    )(x2d, g, w_bf, wdt_bf)


def _diff_attn_kernel(lqk_ref, q_ref, k_ref, v_ref, subln_ref, o_ref, *, tq, lambda_init):
    qi = pl.program_id(2)
    q = q_ref[0] * (HEAD_DIM ** -0.5)
    lane = lax.broadcasted_iota(jnp.int32, (tq, LANES), 1)
    zero = jnp.zeros_like(q)
    qq = jnp.concatenate([jnp.where(lane < HEAD_DIM, q, zero), jnp.where(lane >= HEAD_DIM, q, zero)], axis=0)

    def step(kb, carry, masked):
        m, l, acc = carry
        start = pl.multiple_of(kb * tq, tq)
        k = k_ref[0, pl.ds(start, tq), :]
        v = v_ref[0, pl.ds(start, tq), :]
        s = _dot_nt(qq, k)
        if masked:
            row = lax.broadcasted_iota(jnp.int32, (2 * tq, tq), 0)
            col = lax.broadcasted_iota(jnp.int32, (2 * tq, tq), 1)
            row = jnp.where(row >= tq, row - tq, row)
            s = jnp.where(col <= row, s, -jnp.inf)
        m_new = jnp.maximum(m, jnp.max(s, axis=-1, keepdims=True))
        p = jnp.exp(s - m_new)
        alpha = jnp.exp(m - m_new)
        l = alpha * l + jnp.sum(p, axis=-1, keepdims=True)
        acc = alpha * acc + _dot(p.astype(BF16), v)
        return m_new, l, acc

    init = (jnp.full((2 * tq, 1), -jnp.inf, F32), jnp.zeros((2 * tq, 1), F32), jnp.zeros((2 * tq, LANES), F32))
    carry = lax.fori_loop(0, qi, lambda kb, c: step(kb, c, False), init)
    m, l, acc = step(qi, carry, True)

    lqk = lqk_ref[...]
    lam = (jnp.exp(jnp.sum(lqk[0:1] * lqk[1:2], axis=-1, keepdims=True))
           - jnp.exp(jnp.sum(lqk[2:3] * lqk[3:4], axis=-1, keepdims=True)) + lambda_init)
    o = acc / l
    o = o[:tq] - lam * o[tq:]
    o = _rms(o, subln_ref[...], SUBLN_EPS) * (1.0 - lambda_init)
    o_ref[0] = o.astype(o_ref.dtype)


def _diff_attn(proj, lqk, subln, lambda_init, tq):
    b, s, _ = proj.shape
    h = DIFF_HEADS
    kern = functools.partial(_diff_attn_kernel, tq=tq, lambda_init=lambda_init)
    return pl.pallas_call(
        kern,
        grid=(b, h, s // tq),
        in_specs=[
            pl.BlockSpec((4, HEAD_DIM), lambda bi, hi, qi: (0, 0)),
            pl.BlockSpec((1, tq, LANES), lambda bi, hi, qi: (bi, qi, hi)),
            pl.BlockSpec((1, s, LANES), lambda bi, hi, qi: (bi, 0, h + hi)),
            pl.BlockSpec((1, s, LANES), lambda bi, hi, qi: (bi, 0, 2 * h + hi)),
            pl.BlockSpec((1, LANES), lambda bi, hi, qi: (0, 0)),
        ],
        out_specs=pl.BlockSpec((1, tq, LANES), lambda bi, hi, qi: (bi, qi, hi)),
        out_shape=jax.ShapeDtypeStruct((b, s, h * LANES), BF16),
        compiler_params=pltpu.CompilerParams(
            dimension_semantics=("arbitrary", "arbitrary", "arbitrary"), vmem_limit_bytes=VMEM_LIMIT),
        name="diff_attn",
    )(lqk, proj, proj, proj, subln)


def _sb_attn_kernel(q_ref, k_ref, v_ref, o_ref, *, tq):
    qi = pl.program_id(2)
    q = q_ref[0] * (HEAD_DIM ** -0.5)
    lane = lax.broadcasted_iota(jnp.int32, (tq, LANES), 1)
    row = lax.broadcasted_iota(jnp.int32, (tq, tq), 0)
    col = lax.broadcasted_iota(jnp.int32, (tq, tq), 1)
    strict = col < row
    suffix = jnp.where(row >= col, 1.0, 0.0).astype(BF16)
    zero = jnp.zeros_like(q)

    def block(qh, kb, run, acc, masked):
        start = pl.multiple_of(kb * tq, tq)
        k = k_ref[0, pl.ds(start, tq), :]
        v = v_ref[0, pl.ds(start, tq), :]
        z = _dot_nt(qh, k)
        log_keep = -_softplus(z)
        if masked:
            log_keep = jnp.where(strict, log_keep, 0.0)
        hi, lo = _split2(log_keep)
        csum = _dot(hi, suffix) + _dot(lo, suffix)
        a = jnp.exp(z + csum + run)
        if masked:
            a = jnp.where(strict, a, 0.0)
        acc = acc + _dot(a.astype(BF16), v)
        run = run + csum[:, 0:1]
        return run, acc

    outs = []
    for half in range(2):
        in_half = (lane < HEAD_DIM) if half == 0 else (lane >= HEAD_DIM)
        qh = jnp.where(in_half, q, zero)
        run, acc = block(qh, qi, jnp.zeros((tq, 1), F32), jnp.zeros((tq, LANES), F32), True)

        def cond(st):
            kb, run, _ = st
            return jnp.logical_and(kb >= 0, jnp.max(run) > F32_EXP_ZERO)

        def body(st, qh=qh):
            kb, run, acc = st
            run, acc = block(qh, kb, run, acc, False)
            return kb - 1, run, acc

        _, _, acc = lax.while_loop(cond, body, (qi - 1, run, acc))
        outs.append(acc)
    o_ref[0] = jnp.where(lane < HEAD_DIM, outs[0], outs[1]).astype(o_ref.dtype)


def _sb_attn(proj, tq):
    b, s, _ = proj.shape
    pairs = SB_HEADS * HEAD_DIM // LANES
    base = 3 * DIFF_HEADS * 2 * HEAD_DIM // LANES
    kern = functools.partial(_sb_attn_kernel, tq=tq)
    return pl.pallas_call(
        kern,
        grid=(b, pairs, s // tq),
        in_specs=[
            pl.BlockSpec((1, tq, LANES), lambda bi, hi, qi: (bi, qi, base + hi)),
            pl.BlockSpec((1, s, LANES), lambda bi, hi, qi: (bi, 0, base + pairs + hi)),
            pl.BlockSpec((1, s, LANES), lambda bi, hi, qi: (bi, 0, base + 2 * pairs + hi)),
        ],
        out_specs=pl.BlockSpec((1, tq, LANES), lambda bi, hi, qi: (bi, qi, hi)),
        out_shape=jax.ShapeDtypeStruct((b, s, pairs * LANES), BF16),
        compiler_params=pltpu.CompilerParams(
            dimension_semantics=("arbitrary", "arbitrary", "arbitrary"), vmem_limit_bytes=VMEM_LIMIT),
        name="sb_attn",
    )(proj, proj, proj)


def _mix_ffn_kernel(*refs, n_y, sub, final):
    h_ref = refs[0]
    y_refs = refs[1:1 + n_y]
    wo_refs = refs[1 + n_y:1 + 2 * n_y]
    fg_ref, wg_ref, wu_ref, wd_ref = refs[1 + 2 * n_y:5 + 2 * n_y]
    fin_ref = refs[5 + 2 * n_y] if final else None
    o_ref, n_ref, acc_ref = refs[-3:]
    j = pl.program_id(1)

    @pl.when(j == 0)
    def _():
        h1 = h_ref[...]
        for y_ref, wo_ref in zip(y_refs, wo_refs):
            h1 = h1 + _dot(y_ref[...], wo_ref[...])
        n_ref[...] = _rms(h1, fg_ref[...], NORM_EPS).astype(BF16)
        acc_ref[...] = h1

    n = n_ref[...]
    fc = wg_ref.shape[1]
    acc = acc_ref[...]
    for lo in range(0, fc, sub):
        hi = min(lo + sub, fc)
        g = _dot(n, wg_ref[:, lo:hi])
        u = _dot(n, wu_ref[:, lo:hi])
        a = (g * _sigmoid(g) * u).astype(BF16)
        acc = acc + _dot(a, wd_ref[lo:hi, :])
    acc_ref[...] = acc

    @pl.when(j == pl.num_programs(1) - 1)
    def _():
        out = acc_ref[...]
        if final:
            out = _rms(out, fin_ref[...], NORM_EPS)
        o_ref[...] = out


def _ffn_tiles(f):
    half = f // 2
    fc = half if f % 2 == 0 and half % LANES == 0 else f
    return fc, 2 * LANES


def _mix_ffn(h2d, ys, wos, fg, wg, wu, wd, fin, tm, fc, sub):
    m, d = h2d.shape
    f = wg.shape[1]
    final = fin is not None
    kern = functools.partial(_mix_ffn_kernel, n_y=len(ys), sub=sub, final=final)
    const = lambda shape: pl.BlockSpec(shape, lambda i, j: (0,) * len(shape))
    in_specs = [pl.BlockSpec((tm, d), lambda i, j: (i, 0))]
    in_specs += [pl.BlockSpec((tm, y.shape[1]), lambda i, j: (i, 0)) for y in ys]
    in_specs += [const(w.shape) for w in wos]
    in_specs += [const(fg.shape),
                 pl.BlockSpec((d, fc), lambda i, j: (0, j)),
                 pl.BlockSpec((d, fc), lambda i, j: (0, j)),
                 pl.BlockSpec((fc, d), lambda i, j: (j, 0))]
    args = [h2d, *ys, *wos, fg, wg, wu, wd]
    if final:
        in_specs.append(const(fin.shape))
        args.append(fin)
    return pl.pallas_call(
        kern,
        grid=(m // tm, f // fc),
        in_specs=in_specs,
        out_specs=pl.BlockSpec((tm, d), lambda i, j: (i, 0)),
        out_shape=jax.ShapeDtypeStruct((m, d), F32),
        scratch_shapes=[pltpu.VMEM((tm, d), BF16), pltpu.VMEM((tm, d), F32)],
        compiler_params=pltpu.CompilerParams(
            dimension_semantics=("arbitrary", "arbitrary"), vmem_limit_bytes=VMEM_LIMIT),
        name="mix_ffn",
    )(*args)


def _ssd_kernel(zx_ref, dt_ref, cw_ref, cb_ref, dtb_ref, alog_ref, dsk_ref, gn_ref, exp_ref, o_ref,
                xbuf, state, *, chunk):
    L = chunk
    d_inner = SSD_HEADS * SSD_HEAD_DIM
    gw = d_inner // SSD_GROUPS
    hpg = SSD_HEADS // SSD_GROUPS
    c = pl.program_id(1)

    @pl.when(c == 0)
    def _():
        xbuf[0:SUBLANES, :] = jnp.zeros((SUBLANES, xbuf.shape[1]), F32)
        state[...] = jnp.zeros(state.shape, F32)

    @pl.when(c > 0)
    def _():
        xbuf[0:SUBLANES, :] = xbuf[L:L + SUBLANES, :]

    xbuf[SUBLANES:SUBLANES + L, :] = zx_ref[0, :, d_inner:].astype(F32)
    conv = cb_ref[...]
    for kk in range(SSD_CONV):
        conv = conv + cw_ref[kk:kk + 1, :] * xbuf[SUBLANES - (SSD_CONV - 1) + kk:SUBLANES - (SSD_CONV - 1) + kk + L, :]
    xc = conv * _sigmoid(conv)

    lane = lax.broadcasted_iota(jnp.int32, (1, LANES), 1)
    a = jnp.where(lane < SSD_HEADS, -jnp.exp(alog_ref[...]), 0.0)
    dt = _softplus(dt_ref[0] + dtb_ref[...])
    row = lax.broadcasted_iota(jnp.int32, (L, L), 0)
    col = lax.broadcasted_iota(jnp.int32, (L, L), 1)
    lower = row >= col
    tril = jnp.where(lower, 1.0, 0.0).astype(BF16)
    d_hi, d_mid, d_lo = _split3(dt * a)
    acum = _dot(tril, d_hi) + _dot(tril, d_mid) + _dot(tril, d_lo)
    last = acum[L - 1:L, :]
    ea = jnp.exp(acum)
    w = jnp.exp(last - acum) * dt
    acum_t = acum.T
    dt_t = dt.T
    w_t = w.T
    c_hi, c_mid, c_lo = _split3(jnp.broadcast_to(jnp.exp(last), (SUBLANES, LANES)))
    e = exp_ref[...]
    cd_full = (_dot(c_hi, e) + _dot(c_mid, e) + _dot(c_lo, e))[0:1, :]

    glane = lax.broadcasted_iota(jnp.int32, (1, gw), 1)
    for g in range(SSD_GROUPS):
        xg = xc[:, g * gw:(g + 1) * gw]
        xg_bf = xg.astype(BF16)
        bg = xc[:, d_inner + g * SSD_STATE:d_inner + (g + 1) * SSD_STATE]
        cg = xc[:, d_inner + SSD_GROUPS * SSD_STATE + g * SSD_STATE:
                d_inner + SSD_GROUPS * SSD_STATE + (g + 1) * SSD_STATE]
        cb = _dot_nt(cg.astype(BF16), bg.astype(BF16))
        bg_t = bg.T
        prev = state[g]
        prev_bf = prev.astype(BF16)
        y = dsk_ref[:, g * gw:(g + 1) * gw] * xg
        st = prev * cd_full[:, g * gw:(g + 1) * gw]
        for j in range(hpg):
            h = g * hpg + j
            in_head = jnp.logical_and(glane >= j * SSD_HEAD_DIM, glane < (j + 1) * SSD_HEAD_DIM)
            x_h = jnp.where(in_head, xg_bf, jnp.zeros_like(xg_bf))
            prev_h = jnp.where(in_head, prev_bf, jnp.zeros_like(prev_bf))
            seg = acum[:, h:h + 1] - acum_t[h:h + 1, :]
            decay = jnp.exp(jnp.where(lower, seg, -jnp.inf))
            m_h = (cb * decay * dt_t[h:h + 1, :]).astype(BF16)
            c_h = (cg * ea[:, h:h + 1]).astype(BF16)
            y = y + _dot(m_h, x_h) + _dot(c_h, prev_h)
            st = st + _dot((bg_t * w_t[h:h + 1, :]).astype(BF16), x_h)
        state[g] = st
        z = zx_ref[0, :, g * gw:(g + 1) * gw].astype(F32)
        gated = y * (z * _sigmoid(z))
        o_ref[0, :, g * gw:(g + 1) * gw] = _rms(gated, gn_ref[:, g * gw:(g + 1) * gw], SUBLN_EPS).astype(o_ref.dtype)


def _ssd_scan(zx, dt_raw, conv_w, conv_b, dt_bias, a_log, d_full, gnorm, expand, chunk):
    b, s, width = zx.shape
    d_inner = SSD_HEADS * SSD_HEAD_DIM
    conv_dim = width - d_inner
    kern = functools.partial(_ssd_kernel, chunk=chunk)
    const = lambda shape: pl.BlockSpec(shape, lambda bi, ci: (0,) * len(shape))
    return pl.pallas_call(
        kern,
        grid=(b, s // chunk),
        in_specs=[
            pl.BlockSpec((1, chunk, width), lambda bi, ci: (bi, ci, 0)),
            pl.BlockSpec((1, chunk, LANES), lambda bi, ci: (bi, ci, 0)),
            const(conv_w.shape), const(conv_b.shape), const(dt_bias.shape), const(a_log.shape),
            const(d_full.shape), const(gnorm.shape), const(expand.shape),
        ],
        out_specs=pl.BlockSpec((1, chunk, d_inner), lambda bi, ci: (bi, ci, 0)),
        out_shape=jax.ShapeDtypeStruct((b, s, d_inner), BF16),
        scratch_shapes=[
            pltpu.VMEM((chunk + 2 * SUBLANES, conv_dim), F32),
            pltpu.VMEM((SSD_GROUPS, SSD_STATE, d_inner // SSD_GROUPS), F32),
        ],
        compiler_params=pltpu.CompilerParams(
            dimension_semantics=("arbitrary", "arbitrary"), vmem_limit_bytes=VMEM_LIMIT),
        name="ssd_scan",
    )(zx, dt_raw, conv_w, conv_b, dt_bias, a_log, d_full, gnorm, expand)


def _rope_tables(seq):
    half = HEAD_DIM // 2
    lane = jnp.arange(LANES)
    inv_freq = ROPE_THETA ** (-(lane % half).astype(F32) / half)
    ang = jnp.arange(seq, dtype=F32)[:, None] * inv_freq[None, :]
    sign = jnp.where((lane % HEAD_DIM) < half, -1.0, 1.0)
    return jnp.cos(ang), jnp.sin(ang) * sign[None, :]


def _pad_lanes(v):
    return jnp.pad(v, (0, LANES - v.shape[0]))[None, :]


def _attn_layer(h2d, b, s, norm_w, w_in, lq1, lk1, lq2, lk2, subln, w_out, fg, wg, wu, wd, fin, lambda_init,
                tm, tq_diff, tq_sb):
    cos, sin = _rope_tables(s)
    proj = _attn_proj(h2d, norm_w[None, :], w_in.astype(BF16), cos, sin, s, tm)
    proj = proj.reshape(b, s, -1)
    lqk = jnp.stack([lq1, lk1, lq2, lk2])
    o_diff = _diff_attn(proj, lqk, subln[None, :], lambda_init, tq_diff)
    o_sb = _sb_attn(proj, tq_sb)
    nd = o_diff.shape[-1]
    w_out_bf = w_out.astype(BF16)
    return _mix_ffn(h2d, [o_diff.reshape(b * s, -1), o_sb.reshape(b * s, -1)], [w_out_bf[:nd], w_out_bf[nd:]],
                    fg[None, :], wg.astype(BF16), wu.astype(BF16), wd.astype(BF16), fin, tm, *_ffn_tiles(wg.shape[1]))


def _ssd_layer(h2d, b, s, norm_w, w_in, conv_w, conv_b, dt_bias, a_log, d_skip, gnorm, w_out,
               fg, wg, wu, wd, fin, tm, tn, chunk):
    d_inner = SSD_HEADS * SSD_HEAD_DIM
    main = w_in.shape[1] - SSD_HEADS
    w_bf = w_in.astype(BF16)
    w_dt = jnp.pad(w_bf[:, main:], ((0, 0), (0, LANES - SSD_HEADS)))
    zx, dt_raw = _ssd_proj(h2d, norm_w[None, :], w_bf[:, :main], w_dt, tm, tn)
    expand = (jnp.arange(LANES)[:, None] == (jnp.arange(d_inner)[None, :] // SSD_HEAD_DIM)).astype(BF16)
    y = _ssd_scan(zx.reshape(b, s, main), dt_raw.reshape(b, s, LANES), conv_w, conv_b[None, :],
                  _pad_lanes(dt_bias), _pad_lanes(a_log), jnp.repeat(d_skip, SSD_HEAD_DIM)[None, :],
                  gnorm[None, :], expand, chunk)
    return _mix_ffn(h2d, [y.reshape(b * s, d_inner)], [w_out.astype(BF16)],
                    fg[None, :], wg.astype(BF16), wu.astype(BF16), wd.astype(BF16), fin, tm, *_ffn_tiles(wg.shape[1]))


def kernel(x, attn_norm, attn_w_in, diff_lq1, diff_lk1, diff_lq2, diff_lk2, diff_subln, attn_w_out, ssd_norm, ssd_w_in, ssd_conv_w, ssd_conv_b, ssd_dt_bias, ssd_a_log, ssd_d, ssd_gnorm, ssd_w_out, ffn_norm, ffn_w_gate, ffn_w_up, ffn_w_down, final_norm):
    b, s, d = x.shape
    depth = ffn_norm.shape[0]
    tm = min(512, s)
    h = x.reshape(b * s, d)
    for layer in range(depth):
        i = layer // 2
        fin = final_norm[None, :] if layer == depth - 1 else None
        ffn = (ffn_norm[layer], ffn_w_gate[layer], ffn_w_up[layer], ffn_w_down[layer], fin)
        if layer % 2 == 0:
            lambda_init = 0.8 - 0.6 * math.exp(-0.3 * layer)
            h = _attn_layer(h, b, s, attn_norm[i], attn_w_in[i], diff_lq1[i], diff_lk1[i], diff_lq2[i],
                            diff_lk2[i], diff_subln[i], attn_w_out[i], *ffn, lambda_init,
                            tm=tm, tq_diff=min(512, s), tq_sb=min(256, s))
        else:
            h = _ssd_layer(h, b, s, ssd_norm[i], ssd_w_in[i], ssd_conv_w[i], ssd_conv_b[i], ssd_dt_bias[i],
                           ssd_a_log[i], ssd_d[i], ssd_gnorm[i], ssd_w_out[i], *ffn,
                           tm=tm, tn=1024, chunk=128)
    return h.reshape(b, s, d)
```

```python
import functools
import math

import jax
import jax.numpy as jnp
from jax import lax
from jax.experimental import pallas as pl
from jax.experimental.pallas import tpu as pltpu

F32 = jnp.float32
BF16 = jnp.bfloat16

HEAD_DIM = 64
DIFF_HEADS = 4
SB_HEADS = 8
ROPE_THETA = 10000.0
NORM_EPS = 1e-6
SUBLN_EPS = 1e-5
SSD_HEADS = 32
SSD_GROUPS = 8
SSD_STATE = 128
SSD_CONV = 4
SSD_HEAD_DIM = 64
LANES = 128
SUBLANES = 8
VMEM_LIMIT = 56 * 1024 * 1024

LOG2E = 1.4426950408889634
F32_EXP2_ZERO = -151.0


def _rms(x, g, eps):
    return x * lax.rsqrt(jnp.mean(x * x, axis=-1, keepdims=True) + eps) * g


def _sigmoid(x):
    return 1.0 / (1.0 + jnp.exp(-x))


def _softplus(x):
    return jnp.maximum(x, 0.0) + jnp.log1p(jnp.exp(-jnp.abs(x)))


def _split2(x):
    hi = x.astype(BF16)
    lo = (x - hi.astype(F32)).astype(BF16)
    return hi, lo


def _split3(x):
    hi = x.astype(BF16)
    r = x - hi.astype(F32)
    mid = r.astype(BF16)
    lo = (r - mid.astype(F32)).astype(BF16)
    return hi, mid, lo


def _dot(a, b):
    return jnp.dot(a, b, preferred_element_type=F32)


def _dot_nt(a, b):
    return lax.dot_general(a, b, (((1,), (1,)), ((), ())), preferred_element_type=F32)


MXU_COLS = 256


def _attn_proj_kernel(x_ref, g_ref, w_ref, cos_ref, sin_ref, o_ref, *, rope_cols):
    xn = _rms(x_ref[...], g_ref[...], NORM_EPS).astype(BF16)
    tm = xn.shape[0]
    cos = cos_ref[...]
    sin = sin_ref[...]
    lane = lax.broadcasted_iota(jnp.int32, (tm, LANES), 1)
    first_half = (lane & (HEAD_DIM // 2)) == 0
    for c in range(w_ref.shape[1] // MXU_COLS):
        acc = _dot(xn, w_ref[:, c * MXU_COLS:(c + 1) * MXU_COLS])
        if c * MXU_COLS < rope_cols:
            for s in range(MXU_COLS // LANES):
                t = acc[:, s * LANES:(s + 1) * LANES]
                partner = jnp.where(first_half,
                                    pltpu.roll(t, LANES - HEAD_DIM // 2, 1),
                                    pltpu.roll(t, HEAD_DIM // 2, 1))
                lo = c * MXU_COLS + s * LANES
                o_ref[:, lo:lo + LANES] = (t * cos + partner * sin).astype(o_ref.dtype)
        else:
            o_ref[:, c * MXU_COLS:(c + 1) * MXU_COLS] = acc.astype(o_ref.dtype)


def _attn_proj(x2d, g, w_bf, cos, sin, seq, tm):
    m, d = x2d.shape
    n = w_bf.shape[1]
    rope_cols = 2 * DIFF_HEADS * 2 * HEAD_DIM
    pos_blocks = seq // tm
    return pl.pallas_call(
        functools.partial(_attn_proj_kernel, rope_cols=rope_cols),
        grid=(m // tm,),
        in_specs=[
            pl.BlockSpec((tm, d), lambda i: (i, 0)),
            pl.BlockSpec((1, d), lambda i: (0, 0)),
            pl.BlockSpec((d, n), lambda i: (0, 0)),
            pl.BlockSpec((tm, LANES), lambda i: (i % pos_blocks, 0)),
            pl.BlockSpec((tm, LANES), lambda i: (i % pos_blocks, 0)),
        ],
        out_specs=pl.BlockSpec((tm, n), lambda i: (i, 0)),
        out_shape=jax.ShapeDtypeStruct((m, n), BF16),
        compiler_params=pltpu.CompilerParams(
            dimension_semantics=("arbitrary",), vmem_limit_bytes=VMEM_LIMIT),
        name="attn_proj",
    )(x2d, g, w_bf, cos, sin)


def _ssd_proj_kernel(x_ref, g_ref, w_ref, wdt_ref, o_ref, dt_ref):
    xn = _rms(x_ref[...], g_ref[...], NORM_EPS).astype(BF16)
    dt_ref[...] = _dot(xn, wdt_ref[...])
    for c in range(w_ref.shape[1] // MXU_COLS):
        o_ref[:, c * MXU_COLS:(c + 1) * MXU_COLS] = _dot(
            xn, w_ref[:, c * MXU_COLS:(c + 1) * MXU_COLS]).astype(o_ref.dtype)


def _ssd_proj(x2d, g, w_bf, wdt_bf, tm):
    m, d = x2d.shape
    n = w_bf.shape[1]
    return pl.pallas_call(
        _ssd_proj_kernel,
        grid=(m // tm,),
        in_specs=[
            pl.BlockSpec((tm, d), lambda i: (i, 0)),
            pl.BlockSpec((1, d), lambda i: (0, 0)),
            pl.BlockSpec((d, n), lambda i: (0, 0)),
            pl.BlockSpec((d, LANES), lambda i: (0, 0)),
        ],
        out_specs=[
            pl.BlockSpec((tm, n), lambda i: (i, 0)),
            pl.BlockSpec((tm, LANES), lambda i: (i, 0)),
        ],
        out_shape=[jax.ShapeDtypeStruct((m, n), BF16), jax.ShapeDtypeStruct((m, LANES), F32)],
        compiler_params=pltpu.CompilerParams(
            dimension_semantics=("arbitrary",), vmem_limit_bytes=VMEM_LIMIT),
        name="ssd_proj",
    )(x2d, g, w_bf, wdt_bf)


def _diff_attn_kernel(lqk_ref, q_ref, k_ref, v_ref, subln_ref, o_ref, *, tq, lambda_init):
    qi = pl.program_id(2)
    q = q_ref[0] * (HEAD_DIM ** -0.5)
    lane = lax.broadcasted_iota(jnp.int32, (tq, LANES), 1)
    zero = jnp.zeros_like(q)
    qq = jnp.concatenate([jnp.where(lane < HEAD_DIM, q, zero), jnp.where(lane >= HEAD_DIM, q, zero)], axis=0)

    def step(kb, carry, masked):
        m, l, acc = carry
        start = pl.multiple_of(kb * tq, tq)
        k = k_ref[0, pl.ds(start, tq), :]
        v = v_ref[0, pl.ds(start, tq), :]
        s = _dot_nt(qq, k)
        if masked:
            row = lax.broadcasted_iota(jnp.int32, (2 * tq, tq), 0)
            col = lax.broadcasted_iota(jnp.int32, (2 * tq, tq), 1)
            row = jnp.where(row >= tq, row - tq, row)
            s = jnp.where(col <= row, s, -jnp.inf)
        m_new = jnp.maximum(m, jnp.max(s, axis=-1, keepdims=True))
        p = jnp.exp(s - m_new)
        alpha = jnp.exp(m - m_new)
        l = alpha * l + jnp.sum(p, axis=-1, keepdims=True)
        acc = alpha * acc + _dot(p.astype(BF16), v)
        return m_new, l, acc

    init = (jnp.full((2 * tq, 1), -jnp.inf, F32), jnp.zeros((2 * tq, 1), F32), jnp.zeros((2 * tq, LANES), F32))
    carry = lax.fori_loop(0, qi, lambda kb, c: step(kb, c, False), init)
    m, l, acc = step(qi, carry, True)

    lqk = lqk_ref[...]
    lam = (jnp.exp(jnp.sum(lqk[0:1] * lqk[1:2], axis=-1, keepdims=True))
           - jnp.exp(jnp.sum(lqk[2:3] * lqk[3:4], axis=-1, keepdims=True)) + lambda_init)
    o = acc / l
    o = o[:tq] - lam * o[tq:]
    o = _rms(o, subln_ref[...], SUBLN_EPS) * (1.0 - lambda_init)
    o_ref[0] = o.astype(o_ref.dtype)


def _diff_attn(proj, lqk, subln, lambda_init, tq):
    b, s, _ = proj.shape
    h = DIFF_HEADS
    kern = functools.partial(_diff_attn_kernel, tq=tq, lambda_init=lambda_init)
    return pl.pallas_call(
        kern,
        grid=(b, h, s // tq),
        in_specs=[
            pl.BlockSpec((4, HEAD_DIM), lambda bi, hi, qi: (0, 0)),
            pl.BlockSpec((1, tq, LANES), lambda bi, hi, qi: (bi, qi, hi)),
            pl.BlockSpec((1, s, LANES), lambda bi, hi, qi: (bi, 0, h + hi)),
            pl.BlockSpec((1, s, LANES), lambda bi, hi, qi: (bi, 0, 2 * h + hi)),
            pl.BlockSpec((1, LANES), lambda bi, hi, qi: (0, 0)),
        ],
        out_specs=pl.BlockSpec((1, tq, LANES), lambda bi, hi, qi: (bi, qi, hi)),
        out_shape=jax.ShapeDtypeStruct((b, s, h * LANES), BF16),
        compiler_params=pltpu.CompilerParams(
            dimension_semantics=("arbitrary", "arbitrary", "arbitrary"), vmem_limit_bytes=VMEM_LIMIT),
        name="diff_attn",
    )(lqk, proj, proj, proj, subln)


def _sb_attn_kernel(q_ref, k_ref, v_ref, o_ref, *, tq, tiles):
    first = pl.program_id(2) * tiles
    rows = 2 * tiles * tq
    lane = lax.broadcasted_iota(jnp.int32, (tq, LANES), 1)
    row = lax.broadcasted_iota(jnp.int32, (rows, tq), 0)
    col = lax.broadcasted_iota(jnp.int32, (rows, tq), 1)
    strict = col < (row & (tq - 1))
    suffix = jnp.where(lax.broadcasted_iota(jnp.int32, (tq, tq), 1) <= lax.broadcasted_iota(jnp.int32, (tq, tq), 0),
                       1.0, 0.0).astype(BF16)
    suffix2 = jnp.concatenate([suffix, suffix], axis=0)
    tile_of_row = lax.broadcasted_iota(jnp.int32, (rows, 1), 0) // (2 * tq)

    qqs = []
    for r in range(tiles):
        q = q_ref[0, r * tq:(r + 1) * tq, :] * (HEAD_DIM ** -0.5)
        zero = jnp.zeros_like(q)
        qqs.append(jnp.concatenate([jnp.where(lane < HEAD_DIM, q, zero), jnp.where(lane >= HEAD_DIM, q, zero)], axis=0))

    def step(i, run, acc, masked):
        vs, zs = [], []
        for r in range(tiles):
            start = pl.multiple_of(jnp.maximum(first + r - i, 0) * tq, tq)
            zs.append(_dot_nt(qqs[r], k_ref[0, pl.ds(start, tq), :]))
            vs.append(v_ref[0, pl.ds(start, tq), :])
        z = jnp.concatenate(zs, axis=0) * LOG2E
        log_keep = -(jnp.maximum(z, 0.0) + jnp.log2(1.0 + jnp.exp2(-jnp.abs(z))))
        if masked:
            log_keep = jnp.where(strict, log_keep, 0.0)
        hi, lo = _split2(log_keep)
        csum = _dot(jnp.concatenate([hi, lo], axis=1), suffix2)
        run_in = run if masked else jnp.where(tile_of_row >= i - first, run, -1e30)
        a = jnp.exp2(z + csum + run_in)
        if masked:
            a = jnp.where(strict, a, 0.0)
        a = a.astype(BF16)
        pv = [_dot(a[r * 2 * tq:(r + 1) * 2 * tq], vs[r]) for r in range(tiles)]
        return run_in + csum[:, 0:1], acc + jnp.concatenate(pv, axis=0)

    run, acc = step(0, jnp.zeros((rows, 1), F32), jnp.zeros((rows, LANES), F32), True)

    def cond(st):
        i, run, _ = st
        return jnp.logical_and(first + tiles - 1 - i >= 0, jnp.max(run) > F32_EXP2_ZERO)

    def body(st):
        i, run, acc = st
        run, acc = step(i, run, acc, False)
        return i + 1, run, acc

    _, _, acc = lax.while_loop(cond, body, (jnp.int32(1), run, acc))
    for r in range(tiles):
        o_ref[0, r * tq:(r + 1) * tq, :] = jnp.where(
            lane < HEAD_DIM, acc[2 * r * tq:(2 * r + 1) * tq], acc[(2 * r + 1) * tq:(2 * r + 2) * tq]).astype(o_ref.dtype)


def _sb_attn(proj, tq, tiles):
    b, s, _ = proj.shape
    pairs = SB_HEADS * HEAD_DIM // LANES
    base = 3 * DIFF_HEADS * 2 * HEAD_DIM // LANES
    kern = functools.partial(_sb_attn_kernel, tq=tq, tiles=tiles)
    rows = tq * tiles
    return pl.pallas_call(
        kern,
        grid=(b, pairs, s // rows),
        in_specs=[
            pl.BlockSpec((1, rows, LANES), lambda bi, hi, qi: (bi, qi, base + hi)),
            pl.BlockSpec((1, s, LANES), lambda bi, hi, qi: (bi, 0, base + pairs + hi)),
            pl.BlockSpec((1, s, LANES), lambda bi, hi, qi: (bi, 0, base + 2 * pairs + hi)),
        ],
        out_specs=pl.BlockSpec((1, rows, LANES), lambda bi, hi, qi: (bi, qi, hi)),
        out_shape=jax.ShapeDtypeStruct((b, s, pairs * LANES), BF16),
        compiler_params=pltpu.CompilerParams(
            dimension_semantics=("arbitrary", "arbitrary", "arbitrary"), vmem_limit_bytes=VMEM_LIMIT),
        name="sb_attn",
    )(proj, proj, proj)


def _mix_ffn_kernel(*refs, n_y, sub, final):
    h_ref = refs[0]
    y_refs = refs[1:1 + n_y]
    wo_refs = refs[1 + n_y:1 + 2 * n_y]
    fg_ref, wg_ref, wu_ref, wd_ref = refs[1 + 2 * n_y:5 + 2 * n_y]
    fin_ref = refs[5 + 2 * n_y] if final else None
    o_ref, n_ref, acc_ref = refs[-3:]
    j = pl.program_id(1)

    @pl.when(j == 0)
    def _():
        h1 = h_ref[...]
        for y_ref, wo_ref in zip(y_refs, wo_refs):
            h1 = h1 + _dot(y_ref[...], wo_ref[...])
        n_ref[...] = _rms(h1, fg_ref[...], NORM_EPS).astype(BF16)
        acc_ref[...] = h1

    n = n_ref[...]
    fc = wg_ref.shape[1]
    acc = acc_ref[...]
    for lo in range(0, fc, sub):
        hi = min(lo + sub, fc)
        g = _dot(n, wg_ref[:, lo:hi])
        u = _dot(n, wu_ref[:, lo:hi])
        a = (g * _sigmoid(g) * u).astype(BF16)
        acc = acc + _dot(a, wd_ref[lo:hi, :])
    acc_ref[...] = acc

    @pl.when(j == pl.num_programs(1) - 1)
    def _():
        out = acc_ref[...]
        if final:
            out = _rms(out, fin_ref[...], NORM_EPS)
        o_ref[...] = out


def _ffn_tiles(f):
    half = f // 2
    fc = half if f % 2 == 0 and half % LANES == 0 else f
    return fc, 2 * LANES


def _mix_ffn(h2d, ys, wos, fg, wg, wu, wd, fin, tm, fc, sub):
    m, d = h2d.shape
    f = wg.shape[1]
    final = fin is not None
    kern = functools.partial(_mix_ffn_kernel, n_y=len(ys), sub=sub, final=final)
    const = lambda shape: pl.BlockSpec(shape, lambda i, j: (0,) * len(shape))
    in_specs = [pl.BlockSpec((tm, d), lambda i, j: (i, 0))]
    in_specs += [pl.BlockSpec((tm, y.shape[1]), lambda i, j: (i, 0)) for y in ys]
    in_specs += [const(w.shape) for w in wos]
    in_specs += [const(fg.shape),
                 pl.BlockSpec((d, fc), lambda i, j: (0, j)),
                 pl.BlockSpec((d, fc), lambda i, j: (0, j)),
                 pl.BlockSpec((fc, d), lambda i, j: (j, 0))]
    args = [h2d, *ys, *wos, fg, wg, wu, wd]
    if final:
        in_specs.append(const(fin.shape))
        args.append(fin)
    return pl.pallas_call(
        kern,
        grid=(m // tm, f // fc),
        in_specs=in_specs,
        out_specs=pl.BlockSpec((tm, d), lambda i, j: (i, 0)),
        out_shape=jax.ShapeDtypeStruct((m, d), F32),
        scratch_shapes=[pltpu.VMEM((tm, d), BF16), pltpu.VMEM((tm, d), F32)],
        compiler_params=pltpu.CompilerParams(
            dimension_semantics=("arbitrary", "arbitrary"), vmem_limit_bytes=VMEM_LIMIT),
        name="mix_ffn",
    )(*args)


def _ssd_kernel(zx_ref, dt_ref, cw_ref, cb_ref, dtb_ref, alog_ref, dsk_ref, gn_ref, exp_ref, o_ref,
                xbuf, state, *, chunk):
    L = chunk
    d_inner = SSD_HEADS * SSD_HEAD_DIM
    gw = d_inner // SSD_GROUPS
    hpg = SSD_HEADS // SSD_GROUPS
    c = pl.program_id(1)

    @pl.when(c == 0)
    def _():
        xbuf[0:SUBLANES, :] = jnp.zeros((SUBLANES, xbuf.shape[1]), F32)
        state[...] = jnp.zeros(state.shape, F32)

    @pl.when(c > 0)
    def _():
        xbuf[0:SUBLANES, :] = xbuf[L:L + SUBLANES, :]

    xbuf[SUBLANES:SUBLANES + L, :] = zx_ref[0, :, d_inner:].astype(F32)
    conv = cb_ref[...]
    for kk in range(SSD_CONV):
        conv = conv + cw_ref[kk:kk + 1, :] * xbuf[SUBLANES - (SSD_CONV - 1) + kk:SUBLANES - (SSD_CONV - 1) + kk + L, :]
    xc = conv * _sigmoid(conv)

    lane = lax.broadcasted_iota(jnp.int32, (1, LANES), 1)
    a = jnp.where(lane < SSD_HEADS, -jnp.exp(alog_ref[...]), 0.0)
    dt = _softplus(dt_ref[0] + dtb_ref[...])
    row = lax.broadcasted_iota(jnp.int32, (L, L), 0)
    col = lax.broadcasted_iota(jnp.int32, (L, L), 1)
    lower = row >= col
    tril = jnp.where(lower, 1.0, 0.0).astype(BF16)
    d_hi, d_mid, d_lo = _split3(dt * a)
    acum = _dot(tril, d_hi) + _dot(tril, d_mid) + _dot(tril, d_lo)
    last = acum[L - 1:L, :]
    ea = jnp.exp(acum)
    w = jnp.exp(last - acum) * dt
    acum_t = acum.T
    dt_t = dt.T
    w_t = w.T
    c_hi, c_mid, c_lo = _split3(jnp.broadcast_to(jnp.exp(last), (SUBLANES, LANES)))
    e = exp_ref[...]
    cd_full = (_dot(c_hi, e) + _dot(c_mid, e) + _dot(c_lo, e))[0:1, :]

    glane = lax.broadcasted_iota(jnp.int32, (1, gw), 1)
    for g in range(SSD_GROUPS):
        xg = xc[:, g * gw:(g + 1) * gw]
        xg_bf = xg.astype(BF16)
        bg = xc[:, d_inner + g * SSD_STATE:d_inner + (g + 1) * SSD_STATE]
        cg = xc[:, d_inner + SSD_GROUPS * SSD_STATE + g * SSD_STATE:
                d_inner + SSD_GROUPS * SSD_STATE + (g + 1) * SSD_STATE]
        cb = _dot_nt(cg.astype(BF16), bg.astype(BF16))
        bg_t = bg.T
        prev = state[g]
        prev_bf = prev.astype(BF16)
        y = dsk_ref[:, g * gw:(g + 1) * gw] * xg
        st = prev * cd_full[:, g * gw:(g + 1) * gw]
        for j in range(hpg):
            h = g * hpg + j
            in_head = jnp.logical_and(glane >= j * SSD_HEAD_DIM, glane < (j + 1) * SSD_HEAD_DIM)
            x_h = jnp.where(in_head, xg_bf, jnp.zeros_like(xg_bf))
            prev_h = jnp.where(in_head, prev_bf, jnp.zeros_like(prev_bf))
            seg = acum[:, h:h + 1] - acum_t[h:h + 1, :]
            decay = jnp.exp(jnp.where(lower, seg, -jnp.inf))
            m_h = (cb * decay * dt_t[h:h + 1, :]).astype(BF16)
            c_h = (cg * ea[:, h:h + 1]).astype(BF16)
            y = y + _dot(m_h, x_h) + _dot(c_h, prev_h)
            st = st + _dot((bg_t * w_t[h:h + 1, :]).astype(BF16), x_h)
        state[g] = st
        z = zx_ref[0, :, g * gw:(g + 1) * gw].astype(F32)
        gated = y * (z * _sigmoid(z))
        o_ref[0, :, g * gw:(g + 1) * gw] = _rms(gated, gn_ref[:, g * gw:(g + 1) * gw], SUBLN_EPS).astype(o_ref.dtype)


def _ssd_scan(zx, dt_raw, conv_w, conv_b, dt_bias, a_log, d_full, gnorm, expand, chunk):
    b, s, width = zx.shape
    d_inner = SSD_HEADS * SSD_HEAD_DIM
    conv_dim = width - d_inner
    kern = functools.partial(_ssd_kernel, chunk=chunk)
    const = lambda shape: pl.BlockSpec(shape, lambda bi, ci: (0,) * len(shape))
    return pl.pallas_call(
        kern,
        grid=(b, s // chunk),
        in_specs=[
            pl.BlockSpec((1, chunk, width), lambda bi, ci: (bi, ci, 0)),
            pl.BlockSpec((1, chunk, LANES), lambda bi, ci: (bi, ci, 0)),
            const(conv_w.shape), const(conv_b.shape), const(dt_bias.shape), const(a_log.shape),
            const(d_full.shape), const(gnorm.shape), const(expand.shape),
        ],
        out_specs=pl.BlockSpec((1, chunk, d_inner), lambda bi, ci: (bi, ci, 0)),
        out_shape=jax.ShapeDtypeStruct((b, s, d_inner), BF16),
        scratch_shapes=[
            pltpu.VMEM((chunk + 2 * SUBLANES, conv_dim), F32),
            pltpu.VMEM((SSD_GROUPS, SSD_STATE, d_inner // SSD_GROUPS), F32),
        ],
        compiler_params=pltpu.CompilerParams(
            dimension_semantics=("arbitrary", "arbitrary"), vmem_limit_bytes=VMEM_LIMIT),
        name="ssd_scan",
    )(zx, dt_raw, conv_w, conv_b, dt_bias, a_log, d_full, gnorm, expand)


def _rope_tables(seq):
    half = HEAD_DIM // 2
    lane = jnp.arange(LANES)
    inv_freq = ROPE_THETA ** (-(lane % half).astype(F32) / half)
    ang = jnp.arange(seq, dtype=F32)[:, None] * inv_freq[None, :]
    sign = jnp.where((lane % HEAD_DIM) < half, -1.0, 1.0)
    return jnp.cos(ang), jnp.sin(ang) * sign[None, :]


def _pad_lanes(v):
    return jnp.pad(v, (0, LANES - v.shape[0]))[None, :]


def _attn_layer(h2d, b, s, norm_w, w_in, lq1, lk1, lq2, lk2, subln, w_out, fg, wg, wu, wd, fin, lambda_init,
                tm, tq_diff, tq_sb, sb_tiles):
    cos, sin = _rope_tables(s)
    proj = _attn_proj(h2d, norm_w[None, :], w_in.astype(BF16), cos, sin, s, tm)
    proj = proj.reshape(b, s, -1)
    lqk = jnp.stack([lq1, lk1, lq2, lk2])
    o_diff = _diff_attn(proj, lqk, subln[None, :], lambda_init, tq_diff)
    o_sb = _sb_attn(proj, tq_sb, sb_tiles)
    nd = o_diff.shape[-1]
    w_out_bf = w_out.astype(BF16)
    return _mix_ffn(h2d, [o_diff.reshape(b * s, -1), o_sb.reshape(b * s, -1)], [w_out_bf[:nd], w_out_bf[nd:]],
                    fg[None, :], wg.astype(BF16), wu.astype(BF16), wd.astype(BF16), fin, tm, *_ffn_tiles(wg.shape[1]))


def _ssd_layer(h2d, b, s, norm_w, w_in, conv_w, conv_b, dt_bias, a_log, d_skip, gnorm, w_out,
               fg, wg, wu, wd, fin, tm, chunk):
    d_inner = SSD_HEADS * SSD_HEAD_DIM
    main = w_in.shape[1] - SSD_HEADS
    w_bf = w_in.astype(BF16)
    w_dt = jnp.pad(w_bf[:, main:], ((0, 0), (0, LANES - SSD_HEADS)))
    zx, dt_raw = _ssd_proj(h2d, norm_w[None, :], w_bf[:, :main], w_dt, tm)
    expand = (jnp.arange(LANES)[:, None] == (jnp.arange(d_inner)[None, :] // SSD_HEAD_DIM)).astype(BF16)
    y = _ssd_scan(zx.reshape(b, s, main), dt_raw.reshape(b, s, LANES), conv_w, conv_b[None, :],
                  _pad_lanes(dt_bias), _pad_lanes(a_log), jnp.repeat(d_skip, SSD_HEAD_DIM)[None, :],
                  gnorm[None, :], expand, chunk)
    return _mix_ffn(h2d, [y.reshape(b * s, d_inner)], [w_out.astype(BF16)],
                    fg[None, :], wg.astype(BF16), wu.astype(BF16), wd.astype(BF16), fin, tm, *_ffn_tiles(wg.shape[1]))


def kernel(x, attn_norm, attn_w_in, diff_lq1, diff_lk1, diff_lq2, diff_lk2, diff_subln, attn_w_out, ssd_norm, ssd_w_in, ssd_conv_w, ssd_conv_b, ssd_dt_bias, ssd_a_log, ssd_d, ssd_gnorm, ssd_w_out, ffn_norm, ffn_w_gate, ffn_w_up, ffn_w_down, final_norm):
    b, s, d = x.shape
    depth = ffn_norm.shape[0]
    tm = min(512, s)
    h = x.reshape(b * s, d)
    for layer in range(depth):
        i = layer // 2
        fin = final_norm[None, :] if layer == depth - 1 else None
        ffn = (ffn_norm[layer], ffn_w_gate[layer], ffn_w_up[layer], ffn_w_down[layer], fin)
        if layer % 2 == 0:
            lambda_init = 0.8 - 0.6 * math.exp(-0.3 * layer)
            h = _attn_layer(h, b, s, attn_norm[i], attn_w_in[i], diff_lq1[i], diff_lk1[i], diff_lq2[i],
                            diff_lk2[i], diff_subln[i], attn_w_out[i], *ffn, lambda_init,
                            tm=tm, tq_diff=min(512, s), tq_sb=min(256, s), sb_tiles=2 if s % 512 == 0 else 1)
        else:
            h = _ssd_layer(h, b, s, ssd_norm[i], ssd_w_in[i], ssd_conv_w[i], ssd_conv_b[i], ssd_dt_bias[i],
                           ssd_a_log[i], ssd_d[i], ssd_gnorm[i], ssd_w_out[i], *ffn,
                           tm=tm, chunk=128)
    return h.reshape(b, s, d)
```

```python
import functools
import math

import jax
import jax.numpy as jnp
from jax import lax
from jax.experimental import pallas as pl
from jax.experimental.pallas import tpu as pltpu

F32 = jnp.float32
BF16 = jnp.bfloat16

HEAD_DIM = 64
DIFF_HEADS = 4
SB_HEADS = 8
ROPE_THETA = 10000.0
NORM_EPS = 1e-6
SUBLN_EPS = 1e-5
SSD_HEADS = 32
SSD_GROUPS = 8
SSD_STATE = 128
SSD_CONV = 4
SSD_HEAD_DIM = 64
LANES = 128
SUBLANES = 8
VMEM_LIMIT = 56 * 1024 * 1024

LOG2E = 1.4426950408889634
QUERY_SCALE = LOG2E * HEAD_DIM ** -0.5
F32_EXP2_ZERO = -151.0


def _rms(x, g, eps):
    return x * lax.rsqrt(jnp.mean(x * x, axis=-1, keepdims=True) + eps) * g


def _sigmoid(x):
    return 1.0 / (1.0 + jnp.exp(-x))


def _softplus(x):
    return jnp.maximum(x, 0.0) + jnp.log1p(jnp.exp(-jnp.abs(x)))


def _split2(x):
    hi = x.astype(BF16)
    lo = (x - hi.astype(F32)).astype(BF16)
    return hi, lo


def _split3(x):
    hi = x.astype(BF16)
    r = x - hi.astype(F32)
    mid = r.astype(BF16)
    lo = (r - mid.astype(F32)).astype(BF16)
    return hi, mid, lo


def _dot(a, b):
    return jnp.dot(a, b, preferred_element_type=F32)


def _dot_nt(a, b):
    return lax.dot_general(a, b, (((1,), (1,)), ((), ())), preferred_element_type=F32)


MXU_COLS = 256


def _attn_proj_kernel(x_ref, g_ref, w_ref, cos_ref, sin_ref, o_ref, *, rope_cols, query_cols):
    xn = _rms(x_ref[...], g_ref[...], NORM_EPS).astype(BF16)
    tm = xn.shape[0]
    cos = cos_ref[...]
    sin = sin_ref[...]
    lane = lax.broadcasted_iota(jnp.int32, (tm, LANES), 1)
    first_half = (lane & (HEAD_DIM // 2)) == 0
    for c in range(w_ref.shape[1] // MXU_COLS):
        lo = c * MXU_COLS
        acc = _dot(xn, w_ref[:, lo:lo + MXU_COLS])
        is_query = any(a <= lo < b for a, b in query_cols)
        if lo < rope_cols:
            for s in range(MXU_COLS // LANES):
                t = acc[:, s * LANES:(s + 1) * LANES]
                partner = jnp.where(first_half,
                                    pltpu.roll(t, LANES - HEAD_DIM // 2, 1),
                                    pltpu.roll(t, HEAD_DIM // 2, 1))
                r = t * cos + partner * sin
                if is_query:
                    r = r * QUERY_SCALE
                o_ref[:, lo + s * LANES:lo + (s + 1) * LANES] = r.astype(o_ref.dtype)
        else:
            if is_query:
                acc = acc * QUERY_SCALE
            o_ref[:, lo:lo + MXU_COLS] = acc.astype(o_ref.dtype)


def _attn_proj(x2d, g, w_bf, cos, sin, seq, tm):
    m, d = x2d.shape
    n = w_bf.shape[1]
    diff_w = DIFF_HEADS * 2 * HEAD_DIM
    sb_w = SB_HEADS * HEAD_DIM
    rope_cols = 2 * diff_w
    query_cols = ((0, diff_w), (3 * diff_w, 3 * diff_w + sb_w))
    pos_blocks = seq // tm
    return pl.pallas_call(
        functools.partial(_attn_proj_kernel, rope_cols=rope_cols, query_cols=query_cols),
        grid=(m // tm,),
        in_specs=[
            pl.BlockSpec((tm, d), lambda i: (i, 0)),
            pl.BlockSpec((1, d), lambda i: (0, 0)),
            pl.BlockSpec((d, n), lambda i: (0, 0)),
            pl.BlockSpec((tm, LANES), lambda i: (i % pos_blocks, 0)),
            pl.BlockSpec((tm, LANES), lambda i: (i % pos_blocks, 0)),
        ],
        out_specs=pl.BlockSpec((tm, n), lambda i: (i, 0)),
        out_shape=jax.ShapeDtypeStruct((m, n), BF16),
        compiler_params=pltpu.CompilerParams(
            dimension_semantics=("arbitrary",), vmem_limit_bytes=VMEM_LIMIT),
        name="attn_proj",
    )(x2d, g, w_bf, cos, sin)


def _ssd_proj_kernel(x_ref, g_ref, w_ref, wdt_ref, o_ref, dt_ref):
    xn = _rms(x_ref[...], g_ref[...], NORM_EPS).astype(BF16)
    dt_ref[...] = _dot(xn, wdt_ref[...])
    for c in range(w_ref.shape[1] // MXU_COLS):
        o_ref[:, c * MXU_COLS:(c + 1) * MXU_COLS] = _dot(
            xn, w_ref[:, c * MXU_COLS:(c + 1) * MXU_COLS]).astype(o_ref.dtype)


def _ssd_proj(x2d, g, w_bf, wdt_bf, tm):
    m, d = x2d.shape
    n = w_bf.shape[1]
    return pl.pallas_call(
        _ssd_proj_kernel,
        grid=(m // tm,),
        in_specs=[
            pl.BlockSpec((tm, d), lambda i: (i, 0)),
            pl.BlockSpec((1, d), lambda i: (0, 0)),
            pl.BlockSpec((d, n), lambda i: (0, 0)),
            pl.BlockSpec((d, LANES), lambda i: (0, 0)),
        ],
        out_specs=[
            pl.BlockSpec((tm, n), lambda i: (i, 0)),
            pl.BlockSpec((tm, LANES), lambda i: (i, 0)),
        ],
        out_shape=[jax.ShapeDtypeStruct((m, n), BF16), jax.ShapeDtypeStruct((m, LANES), F32)],
        compiler_params=pltpu.CompilerParams(
            dimension_semantics=("arbitrary",), vmem_limit_bytes=VMEM_LIMIT),
        name="ssd_proj",
    )(x2d, g, w_bf, wdt_bf)


def _diff_attn_kernel(lqk_ref, q_ref, k_ref, v_ref, subln_ref, o_ref, vt_ref, s0_ref, s1_ref, *, tq, lambda_init):
    qi = pl.program_id(2)
    seq = k_ref.shape[1]
    gq = MXU_COLS
    halves = tq // gq

    @pl.when(qi == 0)
    def _():
        for c in range(seq // tq):
            vt_ref[:, c * tq:(c + 1) * tq] = v_ref[0, c * tq:(c + 1) * tq, :].T

    q = q_ref[0]
    lane = lax.broadcasted_iota(jnp.int32, (tq, LANES), 1)
    zero = jnp.zeros_like(q)
    qq = jnp.concatenate([jnp.where(lane < HEAD_DIM, q, zero), jnp.where(lane >= HEAD_DIM, q, zero)], axis=0)
    nq = 2 * tq
    key_i = lax.broadcasted_iota(jnp.int32, (gq, nq), 0)
    qry_i = lax.broadcasted_iota(jnp.int32, (gq, nq), 1) & (tq - 1)

    s_refs = (s0_ref, s1_ref)

    def scores(kb, slot):
        start = pl.multiple_of(kb * gq, gq)
        s_refs[slot][...] = _dot_nt(k_ref[0, pl.ds(start, gq), :], qq)

    def consume(state, kb, slot, diag=None):
        m, l, acc = state
        start = pl.multiple_of(kb * gq, gq)
        s = s_refs[slot][...]
        if diag is not None:
            s = jnp.where(key_i + diag * gq <= qry_i, s, -jnp.inf)
        m_new = jnp.maximum(m, jnp.max(s, axis=0, keepdims=True))
        p = jnp.exp2(s - m_new)
        alpha = jnp.exp2(m - m_new)
        l = alpha * l + jnp.sum(p, axis=0, keepdims=True)
        acc = alpha * acc + _dot(vt_ref[:, pl.ds(start, gq)], p.astype(BF16))
        return m_new, l, acc

    n_full = qi * halves

    def body(j, state):
        kb = j * halves
        scores(kb + 1, 1)
        state = consume(state, kb, 0)
        scores(kb + 2, 0)
        return consume(state, kb + 1, 1)

    scores(0, 0)
    init = (jnp.full((1, nq), -jnp.inf, F32), jnp.zeros((1, nq), F32), jnp.zeros((LANES, nq), F32))
    state = lax.fori_loop(0, qi, body, init)
    scores(n_full + 1, 1)
    state = consume(state, n_full, 0, diag=0)
    state = consume(state, n_full + 1, 1, diag=1)

    lqk = lqk_ref[...]
    lam = (jnp.exp(jnp.sum(lqk[0:1] * lqk[1:2], axis=-1, keepdims=True))
           - jnp.exp(jnp.sum(lqk[2:3] * lqk[3:4], axis=-1, keepdims=True)) + lambda_init)
    _, l, acc = state
    o_t = acc / l
    o_t = o_t[:, :tq] - lam * o_t[:, tq:]
    o = _rms(o_t.T, subln_ref[...], SUBLN_EPS) * (1.0 - lambda_init)
    o_ref[0] = o.astype(o_ref.dtype)


def _diff_attn(proj, lqk, subln, lambda_init, tq):
    b, s, _ = proj.shape
    h = DIFF_HEADS
    assert tq == 2 * MXU_COLS and s % tq == 0, (tq, s)
    kern = functools.partial(_diff_attn_kernel, tq=tq, lambda_init=lambda_init)
    return pl.pallas_call(
        kern,
        grid=(b, h, s // tq),
        in_specs=[
            pl.BlockSpec((4, HEAD_DIM), lambda bi, hi, qi: (0, 0)),
            pl.BlockSpec((1, tq, LANES), lambda bi, hi, qi: (bi, qi, hi)),
            pl.BlockSpec((1, s, LANES), lambda bi, hi, qi: (bi, 0, h + hi)),
            pl.BlockSpec((1, s, LANES), lambda bi, hi, qi: (bi, 0, 2 * h + hi)),
            pl.BlockSpec((1, LANES), lambda bi, hi, qi: (0, 0)),
        ],
        out_specs=pl.BlockSpec((1, tq, LANES), lambda bi, hi, qi: (bi, qi, hi)),
        out_shape=jax.ShapeDtypeStruct((b, s, h * LANES), BF16),
        scratch_shapes=[pltpu.VMEM((LANES, s), BF16),
                        pltpu.VMEM((MXU_COLS, 2 * tq), F32), pltpu.VMEM((MXU_COLS, 2 * tq), F32)],
        compiler_params=pltpu.CompilerParams(
            dimension_semantics=("arbitrary", "arbitrary", "arbitrary"), vmem_limit_bytes=VMEM_LIMIT),
        name="diff_attn",
    )(lqk, proj, proj, proj, subln)


def _sb_attn_kernel(q_ref, k_ref, v_ref, o_ref, *, tq, tiles):
    first = pl.program_id(2) * tiles
    rows = 2 * tiles * tq
    lane = lax.broadcasted_iota(jnp.int32, (tq, LANES), 1)
    row = lax.broadcasted_iota(jnp.int32, (rows, tq), 0)
    col = lax.broadcasted_iota(jnp.int32, (rows, tq), 1)
    strict = col < (row & (tq - 1))
    suffix = jnp.where(lax.broadcasted_iota(jnp.int32, (tq, tq), 1) <= lax.broadcasted_iota(jnp.int32, (tq, tq), 0),
                       1.0, 0.0).astype(BF16)
    suffix2 = jnp.concatenate([suffix, suffix], axis=0)
    tile_of_row = lax.broadcasted_iota(jnp.int32, (rows, 1), 0) // (2 * tq)

    qqs = []
    for r in range(tiles):
        q = q_ref[0, r * tq:(r + 1) * tq, :]
        zero = jnp.zeros_like(q)
        qqs.append(jnp.concatenate([jnp.where(lane < HEAD_DIM, q, zero), jnp.where(lane >= HEAD_DIM, q, zero)], axis=0))

    def step(i, run, acc, masked):
        vs, zs = [], []
        for r in range(tiles):
            start = pl.multiple_of(jnp.maximum(first + r - i, 0) * tq, tq)
            zs.append(_dot_nt(qqs[r], k_ref[0, pl.ds(start, tq), :]))
            vs.append(v_ref[0, pl.ds(start, tq), :])
        z = jnp.concatenate(zs, axis=0)
        log_keep = -(jnp.maximum(z, 0.0) + jnp.log2(1.0 + jnp.exp2(-jnp.abs(z))))
        if masked:
            log_keep = jnp.where(strict, log_keep, 0.0)
        hi, lo = _split2(log_keep)
        csum = _dot(jnp.concatenate([hi, lo], axis=1), suffix2)
        run_in = run if masked else jnp.where(tile_of_row >= i - first, run, -1e30)
        a = jnp.exp2(z + csum + run_in)
        if masked:
            a = jnp.where(strict, a, 0.0)
        a = a.astype(BF16)
        pv = [_dot(a[r * 2 * tq:(r + 1) * 2 * tq], vs[r]) for r in range(tiles)]
        return run_in + csum[:, 0:1], acc + jnp.concatenate(pv, axis=0)

    run, acc = step(0, jnp.zeros((rows, 1), F32), jnp.zeros((rows, LANES), F32), True)

    def cond(st):
        i, run, _ = st
        return jnp.logical_and(first + tiles - 1 - i >= 0, jnp.max(run) > F32_EXP2_ZERO)

    def body(st):
        i, run, acc = st
        run, acc = step(i, run, acc, False)
        return i + 1, run, acc

    _, _, acc = lax.while_loop(cond, body, (jnp.int32(1), run, acc))
    for r in range(tiles):
        o_ref[0, r * tq:(r + 1) * tq, :] = jnp.where(
            lane < HEAD_DIM, acc[2 * r * tq:(2 * r + 1) * tq], acc[(2 * r + 1) * tq:(2 * r + 2) * tq]).astype(o_ref.dtype)


def _sb_attn(proj, tq, tiles):
    b, s, _ = proj.shape
    pairs = SB_HEADS * HEAD_DIM // LANES
    base = 3 * DIFF_HEADS * 2 * HEAD_DIM // LANES
    kern = functools.partial(_sb_attn_kernel, tq=tq, tiles=tiles)
    rows = tq * tiles
    return pl.pallas_call(
        kern,
        grid=(b, pairs, s // rows),
        in_specs=[
            pl.BlockSpec((1, rows, LANES), lambda bi, hi, qi: (bi, qi, base + hi)),
            pl.BlockSpec((1, s, LANES), lambda bi, hi, qi: (bi, 0, base + pairs + hi)),
            pl.BlockSpec((1, s, LANES), lambda bi, hi, qi: (bi, 0, base + 2 * pairs + hi)),
        ],
        out_specs=pl.BlockSpec((1, rows, LANES), lambda bi, hi, qi: (bi, qi, hi)),
        out_shape=jax.ShapeDtypeStruct((b, s, pairs * LANES), BF16),
        compiler_params=pltpu.CompilerParams(
            dimension_semantics=("arbitrary", "arbitrary", "arbitrary"), vmem_limit_bytes=VMEM_LIMIT),
        name="sb_attn",
    )(proj, proj, proj)


def _mix_ffn_kernel(*refs, n_y, sub, final):
    h_ref = refs[0]
    y_refs = refs[1:1 + n_y]
    wo_refs = refs[1 + n_y:1 + 2 * n_y]
    fg_ref, wg_ref, wu_ref, wd_ref = refs[1 + 2 * n_y:5 + 2 * n_y]
    fin_ref = refs[5 + 2 * n_y] if final else None
    o_ref, n_ref, acc_ref = refs[-3:]
    j = pl.program_id(1)

    @pl.when(j == 0)
    def _():
        h1 = h_ref[...]
        for y_ref, wo_ref in zip(y_refs, wo_refs):
            h1 = h1 + _dot(y_ref[...], wo_ref[...])
        n_ref[...] = _rms(h1, fg_ref[...], NORM_EPS).astype(BF16)
        acc_ref[...] = h1

    n = n_ref[...]
    fc = wg_ref.shape[1]
    acc = acc_ref[...]
    for lo in range(0, fc, sub):
        hi = min(lo + sub, fc)
        g = _dot(n, wg_ref[:, lo:hi])
        u = _dot(n, wu_ref[:, lo:hi])
        a = (g * _sigmoid(g) * u).astype(BF16)
        acc = acc + _dot(a, wd_ref[lo:hi, :])
    acc_ref[...] = acc

    @pl.when(j == pl.num_programs(1) - 1)
    def _():
        out = acc_ref[...]
        if final:
            out = _rms(out, fin_ref[...], NORM_EPS)
        o_ref[...] = out


def _ffn_tiles(f):
    half = f // 2
    fc = half if f % 2 == 0 and half % LANES == 0 else f
    return fc, 2 * LANES


def _mix_ffn(h2d, ys, wos, fg, wg, wu, wd, fin, tm, fc, sub):
    m, d = h2d.shape
    f = wg.shape[1]
    final = fin is not None
    kern = functools.partial(_mix_ffn_kernel, n_y=len(ys), sub=sub, final=final)
    const = lambda shape: pl.BlockSpec(shape, lambda i, j: (0,) * len(shape))
    in_specs = [pl.BlockSpec((tm, d), lambda i, j: (i, 0))]
    in_specs += [pl.BlockSpec((tm, y.shape[1]), lambda i, j: (i, 0)) for y in ys]
    in_specs += [const(w.shape) for w in wos]
    in_specs += [const(fg.shape),
                 pl.BlockSpec((d, fc), lambda i, j: (0, j)),
                 pl.BlockSpec((d, fc), lambda i, j: (0, j)),
                 pl.BlockSpec((fc, d), lambda i, j: (j, 0))]
    args = [h2d, *ys, *wos, fg, wg, wu, wd]
    if final:
        in_specs.append(const(fin.shape))
        args.append(fin)
    return pl.pallas_call(
        kern,
        grid=(m // tm, f // fc),
        in_specs=in_specs,
        out_specs=pl.BlockSpec((tm, d), lambda i, j: (i, 0)),
        out_shape=jax.ShapeDtypeStruct((m, d), F32),
        scratch_shapes=[pltpu.VMEM((tm, d), BF16), pltpu.VMEM((tm, d), F32)],
        compiler_params=pltpu.CompilerParams(
            dimension_semantics=("arbitrary", "arbitrary"), vmem_limit_bytes=VMEM_LIMIT),
        name="mix_ffn",
    )(*args)


def _ssd_kernel(zx_ref, dt_ref, cw_ref, cb_ref, dtb_ref, alog_ref, dsk_ref, gn_ref, exp_ref, o_ref,
                xbuf, state, *, chunk):
    L = chunk
    d_inner = SSD_HEADS * SSD_HEAD_DIM
    gw = d_inner // SSD_GROUPS
    hpg = SSD_HEADS // SSD_GROUPS
    c = pl.program_id(1)

    @pl.when(c == 0)
    def _():
        xbuf[0:SUBLANES, :] = jnp.zeros((SUBLANES, xbuf.shape[1]), F32)
        state[...] = jnp.zeros(state.shape, F32)

    @pl.when(c > 0)
    def _():
        xbuf[0:SUBLANES, :] = xbuf[L:L + SUBLANES, :]

    xbuf[SUBLANES:SUBLANES + L, :] = zx_ref[0, :, d_inner:].astype(F32)
    conv = cb_ref[...]
    for kk in range(SSD_CONV):
        conv = conv + cw_ref[kk:kk + 1, :] * xbuf[SUBLANES - (SSD_CONV - 1) + kk:SUBLANES - (SSD_CONV - 1) + kk + L, :]
    xc = conv * _sigmoid(conv)

    lane = lax.broadcasted_iota(jnp.int32, (1, LANES), 1)
    a = jnp.where(lane < SSD_HEADS, -jnp.exp(alog_ref[...]), 0.0)
    dt = _softplus(dt_ref[0] + dtb_ref[...])
    row = lax.broadcasted_iota(jnp.int32, (L, L), 0)
    col = lax.broadcasted_iota(jnp.int32, (L, L), 1)
    lower = row >= col
    tril = jnp.where(lower, 1.0, 0.0).astype(BF16)
    d_hi, d_mid, d_lo = _split3(dt * a)
    acum = _dot(tril, d_hi) + _dot(tril, d_mid) + _dot(tril, d_lo)
    last = acum[L - 1:L, :]
    ea = jnp.exp(acum)
    w = jnp.exp(last - acum) * dt
    acum_t = acum.T
    dt_t = dt.T
    w_t = w.T
    c_hi, c_mid, c_lo = _split3(jnp.broadcast_to(jnp.exp(last), (SUBLANES, LANES)))
    e = exp_ref[...]
    cd_full = (_dot(c_hi, e) + _dot(c_mid, e) + _dot(c_lo, e))[0:1, :]

    glane = lax.broadcasted_iota(jnp.int32, (1, gw), 1)
    for g in range(SSD_GROUPS):
        xg = xc[:, g * gw:(g + 1) * gw]
        xg_bf = xg.astype(BF16)
        bg = xc[:, d_inner + g * SSD_STATE:d_inner + (g + 1) * SSD_STATE]
        cg = xc[:, d_inner + SSD_GROUPS * SSD_STATE + g * SSD_STATE:
                d_inner + SSD_GROUPS * SSD_STATE + (g + 1) * SSD_STATE]
        cb = _dot_nt(cg.astype(BF16), bg.astype(BF16))
        bg_t = bg.T
        prev = state[g]
        prev_bf = prev.astype(BF16)
        y_lhs, y_rhs, st_lhs, st_rhs = [], [], [], []
        for j in range(hpg):
            h = g * hpg + j
            in_head = jnp.logical_and(glane >= j * SSD_HEAD_DIM, glane < (j + 1) * SSD_HEAD_DIM)
            x_h = jnp.where(in_head, xg_bf, jnp.zeros_like(xg_bf))
            prev_h = jnp.where(in_head, prev_bf, jnp.zeros_like(prev_bf))
            seg = acum[:, h:h + 1] - acum_t[h:h + 1, :]
            decay = jnp.exp(jnp.where(lower, seg, -jnp.inf))
            y_lhs += [(cb * decay * dt_t[h:h + 1, :]).astype(BF16), (cg * ea[:, h:h + 1]).astype(BF16)]
            y_rhs += [x_h, prev_h]
            st_lhs.append((bg_t * w_t[h:h + 1, :]).astype(BF16))
            st_rhs.append(x_h)
        y = dsk_ref[:, g * gw:(g + 1) * gw] * xg + _dot(jnp.concatenate(y_lhs, axis=1), jnp.concatenate(y_rhs, axis=0))
        state[g] = prev * cd_full[:, g * gw:(g + 1) * gw] + _dot(
            jnp.concatenate(st_lhs, axis=1), jnp.concatenate(st_rhs, axis=0))
        z = zx_ref[0, :, g * gw:(g + 1) * gw].astype(F32)
        gated = y * (z * _sigmoid(z))
        o_ref[0, :, g * gw:(g + 1) * gw] = _rms(gated, gn_ref[:, g * gw:(g + 1) * gw], SUBLN_EPS).astype(o_ref.dtype)


def _ssd_scan(zx, dt_raw, conv_w, conv_b, dt_bias, a_log, d_full, gnorm, expand, chunk):
    b, s, width = zx.shape
    d_inner = SSD_HEADS * SSD_HEAD_DIM
    conv_dim = width - d_inner
    kern = functools.partial(_ssd_kernel, chunk=chunk)
    const = lambda shape: pl.BlockSpec(shape, lambda bi, ci: (0,) * len(shape))
    return pl.pallas_call(
        kern,
        grid=(b, s // chunk),
        in_specs=[
            pl.BlockSpec((1, chunk, width), lambda bi, ci: (bi, ci, 0)),
            pl.BlockSpec((1, chunk, LANES), lambda bi, ci: (bi, ci, 0)),
            const(conv_w.shape), const(conv_b.shape), const(dt_bias.shape), const(a_log.shape),
            const(d_full.shape), const(gnorm.shape), const(expand.shape),
        ],
        out_specs=pl.BlockSpec((1, chunk, d_inner), lambda bi, ci: (bi, ci, 0)),
        out_shape=jax.ShapeDtypeStruct((b, s, d_inner), BF16),
        scratch_shapes=[
            pltpu.VMEM((chunk + 2 * SUBLANES, conv_dim), F32),
            pltpu.VMEM((SSD_GROUPS, SSD_STATE, d_inner // SSD_GROUPS), F32),
        ],
        compiler_params=pltpu.CompilerParams(
            dimension_semantics=("arbitrary", "arbitrary"), vmem_limit_bytes=VMEM_LIMIT),
        name="ssd_scan",
    )(zx, dt_raw, conv_w, conv_b, dt_bias, a_log, d_full, gnorm, expand)


def _rope_tables(seq):
    half = HEAD_DIM // 2
    lane = jnp.arange(LANES)
    inv_freq = ROPE_THETA ** (-(lane % half).astype(F32) / half)
    ang = jnp.arange(seq, dtype=F32)[:, None] * inv_freq[None, :]
    sign = jnp.where((lane % HEAD_DIM) < half, -1.0, 1.0)
    return jnp.cos(ang), jnp.sin(ang) * sign[None, :]


def _pad_lanes(v):
    return jnp.pad(v, (0, LANES - v.shape[0]))[None, :]


def _attn_layer(h2d, b, s, norm_w, w_in, lq1, lk1, lq2, lk2, subln, w_out, fg, wg, wu, wd, fin, lambda_init,
                tm, tq_diff, tq_sb, sb_tiles):
    cos, sin = _rope_tables(s)
    proj = _attn_proj(h2d, norm_w[None, :], w_in.astype(BF16), cos, sin, s, tm)
    proj = proj.reshape(b, s, -1)
    lqk = jnp.stack([lq1, lk1, lq2, lk2])
    o_diff = _diff_attn(proj, lqk, subln[None, :], lambda_init, tq_diff)
    o_sb = _sb_attn(proj, tq_sb, sb_tiles)
    nd = o_diff.shape[-1]
    w_out_bf = w_out.astype(BF16)
    return _mix_ffn(h2d, [o_diff.reshape(b * s, -1), o_sb.reshape(b * s, -1)], [w_out_bf[:nd], w_out_bf[nd:]],
                    fg[None, :], wg.astype(BF16), wu.astype(BF16), wd.astype(BF16), fin, tm, *_ffn_tiles(wg.shape[1]))


def _ssd_layer(h2d, b, s, norm_w, w_in, conv_w, conv_b, dt_bias, a_log, d_skip, gnorm, w_out,
               fg, wg, wu, wd, fin, tm, chunk):
    d_inner = SSD_HEADS * SSD_HEAD_DIM
    main = w_in.shape[1] - SSD_HEADS
    w_bf = w_in.astype(BF16)
    w_dt = jnp.pad(w_bf[:, main:], ((0, 0), (0, LANES - SSD_HEADS)))
    zx, dt_raw = _ssd_proj(h2d, norm_w[None, :], w_bf[:, :main], w_dt, tm)
    expand = (jnp.arange(LANES)[:, None] == (jnp.arange(d_inner)[None, :] // SSD_HEAD_DIM)).astype(BF16)
    y = _ssd_scan(zx.reshape(b, s, main), dt_raw.reshape(b, s, LANES), conv_w, conv_b[None, :],
                  _pad_lanes(dt_bias), _pad_lanes(a_log), jnp.repeat(d_skip, SSD_HEAD_DIM)[None, :],
                  gnorm[None, :], expand, chunk)
    return _mix_ffn(h2d, [y.reshape(b * s, d_inner)], [w_out.astype(BF16)],
                    fg[None, :], wg.astype(BF16), wu.astype(BF16), wd.astype(BF16), fin, tm, *_ffn_tiles(wg.shape[1]))


def kernel(x, attn_norm, attn_w_in, diff_lq1, diff_lk1, diff_lq2, diff_lk2, diff_subln, attn_w_out, ssd_norm, ssd_w_in, ssd_conv_w, ssd_conv_b, ssd_dt_bias, ssd_a_log, ssd_d, ssd_gnorm, ssd_w_out, ffn_norm, ffn_w_gate, ffn_w_up, ffn_w_down, final_norm):
    b, s, d = x.shape
    depth = ffn_norm.shape[0]
    tm = min(512, s)
    h = x.reshape(b * s, d)
    for layer in range(depth):
        i = layer // 2
        fin = final_norm[None, :] if layer == depth - 1 else None
        ffn = (ffn_norm[layer], ffn_w_gate[layer], ffn_w_up[layer], ffn_w_down[layer], fin)
        if layer % 2 == 0:
            lambda_init = 0.8 - 0.6 * math.exp(-0.3 * layer)
            h = _attn_layer(h, b, s, attn_norm[i], attn_w_in[i], diff_lq1[i], diff_lk1[i], diff_lq2[i],
                            diff_lk2[i], diff_subln[i], attn_w_out[i], *ffn, lambda_init,
                            tm=tm, tq_diff=min(512, s), tq_sb=min(256, s), sb_tiles=2 if s % 512 == 0 else 1)
        else:
            h = _ssd_layer(h, b, s, ssd_norm[i], ssd_w_in[i], ssd_conv_w[i], ssd_conv_b[i], ssd_dt_bias[i],
                           ssd_a_log[i], ssd_d[i], ssd_gnorm[i], ssd_w_out[i], *ffn,
                           tm=tm, chunk=128)
    return h.reshape(b, s, d)
```

```python
import functools
import math

import jax
import jax.numpy as jnp
from jax import lax
from jax.experimental import pallas as pl
from jax.experimental.pallas import tpu as pltpu

F32 = jnp.float32
BF16 = jnp.bfloat16

HEAD_DIM = 64
DIFF_HEADS = 4
SB_HEADS = 8
ROPE_THETA = 10000.0
NORM_EPS = 1e-6
SUBLN_EPS = 1e-5
SSD_HEADS = 32
SSD_GROUPS = 8
SSD_STATE = 128
SSD_CONV = 4
SSD_HEAD_DIM = 64
LANES = 128
SUBLANES = 8
VMEM_LIMIT = 56 * 1024 * 1024

LOG2E = 1.4426950408889634
QUERY_SCALE = LOG2E * HEAD_DIM ** -0.5
F32_EXP2_ZERO = -151.0


def _rms(x, g, eps):
    return x * lax.rsqrt(jnp.mean(x * x, axis=-1, keepdims=True) + eps) * g


def _silu(x):
    h = 0.5 * x
    return h + h * jnp.tanh(h)


def _softplus(x):
    return jnp.maximum(x, 0.0) + jnp.log1p(jnp.exp(-jnp.abs(x)))


def _split2(x):
    hi = x.astype(BF16)
    lo = (x - hi.astype(F32)).astype(BF16)
    return hi, lo


def _split3(x):
    hi = x.astype(BF16)
    r = x - hi.astype(F32)
    mid = r.astype(BF16)
    lo = (r - mid.astype(F32)).astype(BF16)
    return hi, mid, lo


def _dot(a, b):
    return jnp.dot(a, b, preferred_element_type=F32)


def _dot_nt(a, b):
    return lax.dot_general(a, b, (((1,), (1,)), ((), ())), preferred_element_type=F32)


MXU_COLS = 256


def _attn_proj_kernel(x_ref, g_ref, w_ref, cos_ref, sin_ref, o_ref, *, rope_cols, query_cols):
    xn = _rms(x_ref[...], g_ref[...], NORM_EPS).astype(BF16)
    tm = xn.shape[0]
    cos = cos_ref[...]
    sin = sin_ref[...]
    lane = lax.broadcasted_iota(jnp.int32, (tm, LANES), 1)
    first_half = (lane & (HEAD_DIM // 2)) == 0
    for c in range(w_ref.shape[1] // MXU_COLS):
        lo = c * MXU_COLS
        acc = _dot(xn, w_ref[:, lo:lo + MXU_COLS])
        is_query = any(a <= lo < b for a, b in query_cols)
        if lo < rope_cols:
            for s in range(MXU_COLS // LANES):
                t = acc[:, s * LANES:(s + 1) * LANES]
                partner = jnp.where(first_half,
                                    pltpu.roll(t, LANES - HEAD_DIM // 2, 1),
                                    pltpu.roll(t, HEAD_DIM // 2, 1))
                r = t * cos + partner * sin
                if is_query:
                    r = r * QUERY_SCALE
                o_ref[:, lo + s * LANES:lo + (s + 1) * LANES] = r.astype(o_ref.dtype)
        else:
            if is_query:
                acc = acc * QUERY_SCALE
            o_ref[:, lo:lo + MXU_COLS] = acc.astype(o_ref.dtype)


def _attn_proj(x2d, g, w_bf, cos, sin, seq, tm):
    m, d = x2d.shape
    n = w_bf.shape[1]
    diff_w = DIFF_HEADS * 2 * HEAD_DIM
    sb_w = SB_HEADS * HEAD_DIM
    rope_cols = 2 * diff_w
    query_cols = ((0, diff_w), (3 * diff_w, 3 * diff_w + sb_w))
    pos_blocks = seq // tm
    return pl.pallas_call(
        functools.partial(_attn_proj_kernel, rope_cols=rope_cols, query_cols=query_cols),
        grid=(m // tm,),
        in_specs=[
            pl.BlockSpec((tm, d), lambda i: (i, 0)),
            pl.BlockSpec((1, d), lambda i: (0, 0)),
            pl.BlockSpec((d, n), lambda i: (0, 0)),
            pl.BlockSpec((tm, LANES), lambda i: (i % pos_blocks, 0)),
            pl.BlockSpec((tm, LANES), lambda i: (i % pos_blocks, 0)),
        ],
        out_specs=pl.BlockSpec((tm, n), lambda i: (i, 0)),
        out_shape=jax.ShapeDtypeStruct((m, n), BF16),
        compiler_params=pltpu.CompilerParams(
            dimension_semantics=("arbitrary",), vmem_limit_bytes=VMEM_LIMIT),
        name="attn_proj",
    )(x2d, g, w_bf, cos, sin)


def _ssd_proj_kernel(x_ref, g_ref, w_ref, wdt_ref, cw_ref, cb_ref, o_ref, dt_ref, tail_ref, *, z_cols, pos_blocks):
    xn = _rms(x_ref[...], g_ref[...], NORM_EPS).astype(BF16)
    dt_ref[...] = _dot(xn, wdt_ref[...])
    tm = xn.shape[0]
    row8 = lax.broadcasted_iota(jnp.int32, (SUBLANES, MXU_COLS), 0)

    @pl.when(pl.program_id(0) % pos_blocks == 0)
    def _():
        tail_ref[...] = jnp.zeros(tail_ref.shape, F32)

    for c in range(w_ref.shape[1] // MXU_COLS):
        lo = c * MXU_COLS
        acc = _dot(xn, w_ref[:, lo:lo + MXU_COLS])
        if lo < z_cols:
            out = _silu(acc)
        else:
            cl = lo - z_cols
            tail = tail_ref[:, cl:cl + MXU_COLS]
            tail_ref[:, cl:cl + MXU_COLS] = acc[tm - SUBLANES:, :]
            conv = cb_ref[:, cl:cl + MXU_COLS] + cw_ref[SSD_CONV - 1:SSD_CONV, cl:cl + MXU_COLS] * acc
            for d in range(1, SSD_CONV):
                rolled = pltpu.roll(acc, d, 0)
                top = jnp.where(row8 < d, pltpu.roll(tail, d, 0), rolled[:SUBLANES])
                shifted = jnp.concatenate([top, rolled[SUBLANES:]], axis=0)
                conv = conv + cw_ref[SSD_CONV - 1 - d:SSD_CONV - d, cl:cl + MXU_COLS] * shifted
            out = _silu(conv)
        o_ref[:, lo:lo + MXU_COLS] = out.astype(o_ref.dtype)


def _ssd_proj(x2d, g, w_bf, wdt_bf, conv_w, conv_b, seq, tm):
    m, d = x2d.shape
    n = w_bf.shape[1]
    conv_dim = conv_w.shape[1]
    kern = functools.partial(_ssd_proj_kernel, z_cols=n - conv_dim, pos_blocks=seq // tm)
    return pl.pallas_call(
        kern,
        grid=(m // tm,),
        in_specs=[
            pl.BlockSpec((tm, d), lambda i: (i, 0)),
            pl.BlockSpec((1, d), lambda i: (0, 0)),
            pl.BlockSpec((d, n), lambda i: (0, 0)),
            pl.BlockSpec((d, LANES), lambda i: (0, 0)),
            pl.BlockSpec(conv_w.shape, lambda i: (0, 0)),
            pl.BlockSpec(conv_b.shape, lambda i: (0, 0)),
        ],
        out_specs=[
            pl.BlockSpec((tm, n), lambda i: (i, 0)),
            pl.BlockSpec((tm, LANES), lambda i: (i, 0)),
        ],
        out_shape=[jax.ShapeDtypeStruct((m, n), BF16), jax.ShapeDtypeStruct((m, LANES), F32)],
        scratch_shapes=[pltpu.VMEM((SUBLANES, conv_dim), F32)],
        compiler_params=pltpu.CompilerParams(
            dimension_semantics=("arbitrary",), vmem_limit_bytes=VMEM_LIMIT),
        name="ssd_proj",
    )(x2d, g, w_bf, wdt_bf, conv_w, conv_b)


def _diff_attn_kernel(lqk_ref, q_ref, k_ref, v_ref, subln_ref, o_ref, vt_ref, s0_ref, s1_ref, *, tq, lambda_init):
    qi = pl.program_id(2)
    seq = k_ref.shape[1]
    gq = MXU_COLS
    halves = tq // gq

    @pl.when(qi == 0)
    def _():
        for c in range(seq // tq):
            vt_ref[:, c * tq:(c + 1) * tq] = v_ref[0, c * tq:(c + 1) * tq, :].T

    q = q_ref[0]
    lane = lax.broadcasted_iota(jnp.int32, (tq, LANES), 1)
    zero = jnp.zeros_like(q)
    qq = jnp.concatenate([jnp.where(lane < HEAD_DIM, q, zero), jnp.where(lane >= HEAD_DIM, q, zero)], axis=0)
    nq = 2 * tq
    key_i = lax.broadcasted_iota(jnp.int32, (gq, nq), 0)
    qry_i = lax.broadcasted_iota(jnp.int32, (gq, nq), 1) & (tq - 1)

    s_refs = (s0_ref, s1_ref)

    def scores(kb, slot):
        start = pl.multiple_of(kb * gq, gq)
        s_refs[slot][...] = _dot_nt(k_ref[0, pl.ds(start, gq), :], qq)

    def consume(state, kb, slot, diag=None):
        m, l, acc = state
        start = pl.multiple_of(kb * gq, gq)
        s = s_refs[slot][...]
        if diag is not None:
            s = jnp.where(key_i + diag * gq <= qry_i, s, -jnp.inf)
        m_new = jnp.maximum(m, jnp.max(s, axis=0, keepdims=True))
        p = jnp.exp2(s - m_new)
        alpha = jnp.exp2(m - m_new)
        l = alpha * l + jnp.sum(p, axis=0, keepdims=True)
        acc = alpha * acc + _dot(vt_ref[:, pl.ds(start, gq)], p.astype(BF16))
        return m_new, l, acc

    n_full = qi * halves

    def body(j, state):
        kb = j * halves
        scores(kb + 1, 1)
        state = consume(state, kb, 0)
        scores(kb + 2, 0)
        return consume(state, kb + 1, 1)

    scores(0, 0)
    init = (jnp.full((1, nq), -jnp.inf, F32), jnp.zeros((1, nq), F32), jnp.zeros((LANES, nq), F32))
    state = lax.fori_loop(0, qi, body, init)
    scores(n_full + 1, 1)
    state = consume(state, n_full, 0, diag=0)
    state = consume(state, n_full + 1, 1, diag=1)

    lqk = lqk_ref[...]
    lam = (jnp.exp(jnp.sum(lqk[0:1] * lqk[1:2], axis=-1, keepdims=True))
           - jnp.exp(jnp.sum(lqk[2:3] * lqk[3:4], axis=-1, keepdims=True)) + lambda_init)
    _, l, acc = state
    o_t = acc / l
    o_t = o_t[:, :tq] - lam * o_t[:, tq:]
    o = _rms(o_t.T, subln_ref[...], SUBLN_EPS) * (1.0 - lambda_init)
    o_ref[0] = o.astype(o_ref.dtype)


def _diff_attn(proj, lqk, subln, lambda_init, tq):
    b, s, _ = proj.shape
    h = DIFF_HEADS
    assert tq == 2 * MXU_COLS and s % tq == 0, (tq, s)
    kern = functools.partial(_diff_attn_kernel, tq=tq, lambda_init=lambda_init)
    return pl.pallas_call(
        kern,
        grid=(b, h, s // tq),
        in_specs=[
            pl.BlockSpec((4, HEAD_DIM), lambda bi, hi, qi: (0, 0)),
            pl.BlockSpec((1, tq, LANES), lambda bi, hi, qi: (bi, qi, hi)),
            pl.BlockSpec((1, s, LANES), lambda bi, hi, qi: (bi, 0, h + hi)),
            pl.BlockSpec((1, s, LANES), lambda bi, hi, qi: (bi, 0, 2 * h + hi)),
            pl.BlockSpec((1, LANES), lambda bi, hi, qi: (0, 0)),
        ],
        out_specs=pl.BlockSpec((1, tq, LANES), lambda bi, hi, qi: (bi, qi, hi)),
        out_shape=jax.ShapeDtypeStruct((b, s, h * LANES), BF16),
        scratch_shapes=[pltpu.VMEM((LANES, s), BF16),
                        pltpu.VMEM((MXU_COLS, 2 * tq), F32), pltpu.VMEM((MXU_COLS, 2 * tq), F32)],
        compiler_params=pltpu.CompilerParams(
            dimension_semantics=("arbitrary", "arbitrary", "arbitrary"), vmem_limit_bytes=VMEM_LIMIT),
        name="diff_attn",
    )(lqk, proj, proj, proj, subln)


def _sb_attn_kernel(q_ref, k_ref, v_ref, o_ref, *, tq, tiles):
    first = pl.program_id(2) * tiles
    rows = 2 * tiles * tq
    lane = lax.broadcasted_iota(jnp.int32, (tq, LANES), 1)
    row = lax.broadcasted_iota(jnp.int32, (rows, tq), 0)
    col = lax.broadcasted_iota(jnp.int32, (rows, tq), 1)
    strict = col < (row & (tq - 1))
    suffix = jnp.where(lax.broadcasted_iota(jnp.int32, (tq, tq), 1) <= lax.broadcasted_iota(jnp.int32, (tq, tq), 0),
                       1.0, 0.0).astype(BF16)
    suffix2 = jnp.concatenate([suffix, suffix], axis=0)
    tile_of_row = lax.broadcasted_iota(jnp.int32, (rows, 1), 0) // (2 * tq)

    qqs = []
    for r in range(tiles):
        q = q_ref[0, r * tq:(r + 1) * tq, :]
        zero = jnp.zeros_like(q)
        qqs.append(jnp.concatenate([jnp.where(lane < HEAD_DIM, q, zero), jnp.where(lane >= HEAD_DIM, q, zero)], axis=0))

    def step(i, run, acc, masked):
        vs, zs = [], []
        for r in range(tiles):
            start = pl.multiple_of(jnp.maximum(first + r - i, 0) * tq, tq)
            zs.append(_dot_nt(qqs[r], k_ref[0, pl.ds(start, tq), :]))
            vs.append(v_ref[0, pl.ds(start, tq), :])
        z = jnp.concatenate(zs, axis=0)
        log_keep = -(jnp.maximum(z, 0.0) + jnp.log2(1.0 + jnp.exp2(-jnp.abs(z))))
        if masked:
            log_keep = jnp.where(strict, log_keep, 0.0)
        hi, lo = _split2(log_keep)
        csum = _dot(jnp.concatenate([hi, lo], axis=1), suffix2)
        run_in = run if masked else jnp.where(tile_of_row >= i - first, run, -1e30)
        a = jnp.exp2(z + csum + run_in)
        if masked:
            a = jnp.where(strict, a, 0.0)
        a = a.astype(BF16)
        pv = [_dot(a[r * 2 * tq:(r + 1) * 2 * tq], vs[r]) for r in range(tiles)]
        return run_in + csum[:, 0:1], acc + jnp.concatenate(pv, axis=0)

    run, acc = step(0, jnp.zeros((rows, 1), F32), jnp.zeros((rows, LANES), F32), True)
    run, acc = step(1, run, acc, False)

    def cond(st):
        i, run, _ = st
        return jnp.logical_and(first + tiles - 1 - i >= 0, jnp.max(run) > F32_EXP2_ZERO)

    def body(st):
        i, run, acc = st
        run, acc = step(i, run, acc, False)
        return i + 1, run, acc

    _, _, acc = lax.while_loop(cond, body, (jnp.int32(2), run, acc))
    for r in range(tiles):
        o_ref[0, r * tq:(r + 1) * tq, :] = jnp.where(
            lane < HEAD_DIM, acc[2 * r * tq:(2 * r + 1) * tq], acc[(2 * r + 1) * tq:(2 * r + 2) * tq]).astype(o_ref.dtype)


def _sb_attn(proj, tq, tiles):
    b, s, _ = proj.shape
    pairs = SB_HEADS * HEAD_DIM // LANES
    base = 3 * DIFF_HEADS * 2 * HEAD_DIM // LANES
    kern = functools.partial(_sb_attn_kernel, tq=tq, tiles=tiles)
    rows = tq * tiles
    return pl.pallas_call(
        kern,
        grid=(b, pairs, s // rows),
        in_specs=[
            pl.BlockSpec((1, rows, LANES), lambda bi, hi, qi: (bi, qi, base + hi)),
            pl.BlockSpec((1, s, LANES), lambda bi, hi, qi: (bi, 0, base + pairs + hi)),
            pl.BlockSpec((1, s, LANES), lambda bi, hi, qi: (bi, 0, base + 2 * pairs + hi)),
        ],
        out_specs=pl.BlockSpec((1, rows, LANES), lambda bi, hi, qi: (bi, qi, hi)),
        out_shape=jax.ShapeDtypeStruct((b, s, pairs * LANES), BF16),
        compiler_params=pltpu.CompilerParams(
            dimension_semantics=("arbitrary", "arbitrary", "arbitrary"), vmem_limit_bytes=VMEM_LIMIT),
        name="sb_attn",
    )(proj, proj, proj)


def _mix_ffn_kernel(*refs, n_y, sub, final):
    h_ref = refs[0]
    y_refs = refs[1:1 + n_y]
    wo_refs = refs[1 + n_y:1 + 2 * n_y]
    fg_ref, wg_ref, wu_ref, wd_ref = refs[1 + 2 * n_y:5 + 2 * n_y]
    fin_ref = refs[5 + 2 * n_y] if final else None
    o_ref, n_ref, acc_ref = refs[-3:]
    j = pl.program_id(1)

    @pl.when(j == 0)
    def _():
        h1 = h_ref[...]
        for y_ref, wo_ref in zip(y_refs, wo_refs):
            h1 = h1 + _dot(y_ref[...], wo_ref[...])
        n_ref[...] = _rms(h1, fg_ref[...], NORM_EPS).astype(BF16)
        acc_ref[...] = h1

    n = n_ref[...]
    fc = wg_ref.shape[1]
    acc = acc_ref[...]
    for lo in range(0, fc, sub):
        hi = min(lo + sub, fc)
        g = _dot(n, wg_ref[:, lo:hi])
        u = _dot(n, wu_ref[:, lo:hi])
        a = (_silu(g) * u).astype(BF16)
        acc = acc + _dot(a, wd_ref[lo:hi, :])
    acc_ref[...] = acc

    @pl.when(j == pl.num_programs(1) - 1)
    def _():
        out = acc_ref[...]
        if final:
            out = _rms(out, fin_ref[...], NORM_EPS)
        o_ref[...] = out


def _ffn_tiles(f):
    half = f // 2
    fc = half if f % 2 == 0 and half % LANES == 0 else f
    return fc, 2 * LANES


def _mix_ffn(h2d, ys, wos, fg, wg, wu, wd, fin, tm, fc, sub):
    m, d = h2d.shape
    f = wg.shape[1]
    final = fin is not None
    kern = functools.partial(_mix_ffn_kernel, n_y=len(ys), sub=sub, final=final)
    const = lambda shape: pl.BlockSpec(shape, lambda i, j: (0,) * len(shape))
    in_specs = [pl.BlockSpec((tm, d), lambda i, j: (i, 0))]
    in_specs += [pl.BlockSpec((tm, y.shape[1]), lambda i, j: (i, 0)) for y in ys]
    in_specs += [const(w.shape) for w in wos]
    in_specs += [const(fg.shape),
                 pl.BlockSpec((d, fc), lambda i, j: (0, j)),
                 pl.BlockSpec((d, fc), lambda i, j: (0, j)),
                 pl.BlockSpec((fc, d), lambda i, j: (j, 0))]
    args = [h2d, *ys, *wos, fg, wg, wu, wd]
    if final:
        in_specs.append(const(fin.shape))
        args.append(fin)
    return pl.pallas_call(
        kern,
        grid=(m // tm, f // fc),
        in_specs=in_specs,
        out_specs=pl.BlockSpec((tm, d), lambda i, j: (i, 0)),
        out_shape=jax.ShapeDtypeStruct((m, d), F32),
        scratch_shapes=[pltpu.VMEM((tm, d), BF16), pltpu.VMEM((tm, d), F32)],
        compiler_params=pltpu.CompilerParams(
            dimension_semantics=("arbitrary", "arbitrary"), vmem_limit_bytes=VMEM_LIMIT),
        name="mix_ffn",
    )(*args)


def _ssd_kernel(zx_ref, dt_ref, dtb_ref, alog_ref, dsk_ref, gn_ref, exp_ref, o_ref, state, *, chunk):
    L = chunk
    d_inner = SSD_HEADS * SSD_HEAD_DIM
    gw = d_inner // SSD_GROUPS
    hpg = SSD_HEADS // SSD_GROUPS
    x_off, b_off, c_off = d_inner, 2 * d_inner, 2 * d_inner + SSD_GROUPS * SSD_STATE

    @pl.when(pl.program_id(1) == 0)
    def _():
        state[...] = jnp.zeros(state.shape, F32)

    lane = lax.broadcasted_iota(jnp.int32, (1, LANES), 1)
    a = jnp.where(lane < SSD_HEADS, -jnp.exp(alog_ref[...]), 0.0)
    dt = _softplus(dt_ref[0] + dtb_ref[...])
    row = lax.broadcasted_iota(jnp.int32, (L, L), 0)
    col = lax.broadcasted_iota(jnp.int32, (L, L), 1)
    lower = row >= col
    tril = jnp.where(lower, 1.0, 0.0).astype(BF16)
    d_hi, d_mid, d_lo = _split3(dt * a)
    acum = _dot(tril, d_hi) + _dot(tril, d_mid) + _dot(tril, d_lo)
    last = acum[L - 1:L, :]
    ea = jnp.exp(acum)
    w = jnp.exp(last - acum) * dt
    acum_t = acum.T
    dt_t = dt.T
    w_t = w.T
    c_hi, c_mid, c_lo = _split3(jnp.broadcast_to(jnp.exp(last), (SUBLANES, LANES)))
    e = exp_ref[...]
    cd_full = (_dot(c_hi, e) + _dot(c_mid, e) + _dot(c_lo, e))[0:1, :]

    glane = lax.broadcasted_iota(jnp.int32, (1, gw), 1)
    for g in range(SSD_GROUPS):
        xg_bf = zx_ref[0, :, x_off + g * gw:x_off + (g + 1) * gw]
        xg = xg_bf.astype(F32)
        bg_bf = zx_ref[0, :, b_off + g * SSD_STATE:b_off + (g + 1) * SSD_STATE]
        cg_bf = zx_ref[0, :, c_off + g * SSD_STATE:c_off + (g + 1) * SSD_STATE]
        cg = cg_bf.astype(F32)
        cb = _dot_nt(cg_bf, bg_bf)
        bg_t = bg_bf.astype(F32).T
        prev = state[g]
        prev_bf = prev.astype(BF16)
        y_lhs, y_rhs, st_lhs, st_rhs = [], [], [], []
        for j in range(hpg):
            h = g * hpg + j
            in_head = jnp.logical_and(glane >= j * SSD_HEAD_DIM, glane < (j + 1) * SSD_HEAD_DIM)
            x_h = jnp.where(in_head, xg_bf, jnp.zeros_like(xg_bf))
            prev_h = jnp.where(in_head, prev_bf, jnp.zeros_like(prev_bf))
            seg = acum[:, h:h + 1] - acum_t[h:h + 1, :]
            decay = jnp.exp(jnp.where(lower, seg, -jnp.inf))
            y_lhs += [(cb * decay * dt_t[h:h + 1, :]).astype(BF16), (cg * ea[:, h:h + 1]).astype(BF16)]
            y_rhs += [x_h, prev_h]
            st_lhs.append((bg_t * w_t[h:h + 1, :]).astype(BF16))
            st_rhs.append(x_h)
        y = dsk_ref[:, g * gw:(g + 1) * gw] * xg + _dot(jnp.concatenate(y_lhs, axis=1), jnp.concatenate(y_rhs, axis=0))
        state[g] = prev * cd_full[:, g * gw:(g + 1) * gw] + _dot(
            jnp.concatenate(st_lhs, axis=1), jnp.concatenate(st_rhs, axis=0))
        gated = y * zx_ref[0, :, g * gw:(g + 1) * gw].astype(F32)
        o_ref[0, :, g * gw:(g + 1) * gw] = _rms(gated, gn_ref[:, g * gw:(g + 1) * gw], SUBLN_EPS).astype(o_ref.dtype)


def _ssd_scan(zx, dt_raw, dt_bias, a_log, d_full, gnorm, expand, chunk):
    b, s, width = zx.shape
    d_inner = SSD_HEADS * SSD_HEAD_DIM
    kern = functools.partial(_ssd_kernel, chunk=chunk)
    const = lambda shape: pl.BlockSpec(shape, lambda bi, ci: (0,) * len(shape))
    return pl.pallas_call(
        kern,
        grid=(b, s // chunk),
        in_specs=[
            pl.BlockSpec((1, chunk, width), lambda bi, ci: (bi, ci, 0)),
            pl.BlockSpec((1, chunk, LANES), lambda bi, ci: (bi, ci, 0)),
            const(dt_bias.shape), const(a_log.shape),
            const(d_full.shape), const(gnorm.shape), const(expand.shape),
        ],
        out_specs=pl.BlockSpec((1, chunk, d_inner), lambda bi, ci: (bi, ci, 0)),
        out_shape=jax.ShapeDtypeStruct((b, s, d_inner), BF16),
        scratch_shapes=[pltpu.VMEM((SSD_GROUPS, SSD_STATE, d_inner // SSD_GROUPS), F32)],
        compiler_params=pltpu.CompilerParams(
            dimension_semantics=("arbitrary", "arbitrary"), vmem_limit_bytes=VMEM_LIMIT),
        name="ssd_scan",
    )(zx, dt_raw, dt_bias, a_log, d_full, gnorm, expand)


def _rope_tables(seq):
    half = HEAD_DIM // 2
    lane = jnp.arange(LANES)
    inv_freq = ROPE_THETA ** (-(lane % half).astype(F32) / half)
    ang = jnp.arange(seq, dtype=F32)[:, None] * inv_freq[None, :]
    sign = jnp.where((lane % HEAD_DIM) < half, -1.0, 1.0)
    return jnp.cos(ang), jnp.sin(ang) * sign[None, :]


def _pad_lanes(v):
    return jnp.pad(v, (0, LANES - v.shape[0]))[None, :]


def _attn_layer(h2d, b, s, norm_w, w_in, lq1, lk1, lq2, lk2, subln, w_out, fg, wg, wu, wd, fin, lambda_init,
                tm, tq_diff, tq_sb, sb_tiles):
    cos, sin = _rope_tables(s)
    proj = _attn_proj(h2d, norm_w[None, :], w_in.astype(BF16), cos, sin, s, tm)
    proj = proj.reshape(b, s, -1)
    lqk = jnp.stack([lq1, lk1, lq2, lk2])
    o_diff = _diff_attn(proj, lqk, subln[None, :], lambda_init, tq_diff)
    o_sb = _sb_attn(proj, tq_sb, sb_tiles)
    nd = o_diff.shape[-1]
    w_out_bf = w_out.astype(BF16)
    return _mix_ffn(h2d, [o_diff.reshape(b * s, -1), o_sb.reshape(b * s, -1)], [w_out_bf[:nd], w_out_bf[nd:]],
                    fg[None, :], wg.astype(BF16), wu.astype(BF16), wd.astype(BF16), fin, tm, *_ffn_tiles(wg.shape[1]))


def _ssd_layer(h2d, b, s, norm_w, w_in, conv_w, conv_b, dt_bias, a_log, d_skip, gnorm, w_out,
               fg, wg, wu, wd, fin, tm, chunk):
    d_inner = SSD_HEADS * SSD_HEAD_DIM
    main = w_in.shape[1] - SSD_HEADS
    w_bf = w_in.astype(BF16)
    w_dt = jnp.pad(w_bf[:, main:], ((0, 0), (0, LANES - SSD_HEADS)))
    zx, dt_raw = _ssd_proj(h2d, norm_w[None, :], w_bf[:, :main], w_dt, conv_w, conv_b[None, :], s, tm)
    expand = (jnp.arange(LANES)[:, None] == (jnp.arange(d_inner)[None, :] // SSD_HEAD_DIM)).astype(BF16)
    y = _ssd_scan(zx.reshape(b, s, main), dt_raw.reshape(b, s, LANES),
                  _pad_lanes(dt_bias), _pad_lanes(a_log), jnp.repeat(d_skip, SSD_HEAD_DIM)[None, :],
                  gnorm[None, :], expand, chunk)
    return _mix_ffn(h2d, [y.reshape(b * s, d_inner)], [w_out.astype(BF16)],
                    fg[None, :], wg.astype(BF16), wu.astype(BF16), wd.astype(BF16), fin, tm, *_ffn_tiles(wg.shape[1]))


def kernel(x, attn_norm, attn_w_in, diff_lq1, diff_lk1, diff_lq2, diff_lk2, diff_subln, attn_w_out, ssd_norm, ssd_w_in, ssd_conv_w, ssd_conv_b, ssd_dt_bias, ssd_a_log, ssd_d, ssd_gnorm, ssd_w_out, ffn_norm, ffn_w_gate, ffn_w_up, ffn_w_down, final_norm):
    b, s, d = x.shape
    depth = ffn_norm.shape[0]
    tm = min(512, s)
    h = x.reshape(b * s, d)
    for layer in range(depth):
        i = layer // 2
        fin = final_norm[None, :] if layer == depth - 1 else None
        ffn = (ffn_norm[layer], ffn_w_gate[layer], ffn_w_up[layer], ffn_w_down[layer], fin)
        if layer % 2 == 0:
            lambda_init = 0.8 - 0.6 * math.exp(-0.3 * layer)
            h = _attn_layer(h, b, s, attn_norm[i], attn_w_in[i], diff_lq1[i], diff_lk1[i], diff_lq2[i],
                            diff_lk2[i], diff_subln[i], attn_w_out[i], *ffn, lambda_init,
                            tm=tm, tq_diff=min(512, s), tq_sb=min(256, s), sb_tiles=2 if s % 512 == 0 else 1)
        else:
            h = _ssd_layer(h, b, s, ssd_norm[i], ssd_w_in[i], ssd_conv_w[i], ssd_conv_b[i], ssd_dt_bias[i],
                           ssd_a_log[i], ssd_d[i], ssd_gnorm[i], ssd_w_out[i], *ffn,
                           tm=tm, chunk=128)
    return h.reshape(b, s, d)
```

```python
import functools
import math

import jax
import jax.numpy as jnp
from jax import lax
from jax.experimental import pallas as pl
from jax.experimental.pallas import tpu as pltpu

F32 = jnp.float32
BF16 = jnp.bfloat16

HEAD_DIM = 64
DIFF_HEADS = 4
SB_HEADS = 8
ROPE_THETA = 10000.0
NORM_EPS = 1e-6
SUBLN_EPS = 1e-5
SSD_HEADS = 32
SSD_GROUPS = 8
SSD_STATE = 128
SSD_CONV = 4
SSD_HEAD_DIM = 64
LANES = 128
SUBLANES = 8
VMEM_LIMIT = 56 * 1024 * 1024

LOG2E = 1.4426950408889634
QUERY_SCALE = LOG2E * HEAD_DIM ** -0.5
F32_EXP2_ZERO = -151.0


def _rms(x, g, eps):
    return x * lax.rsqrt(jnp.mean(x * x, axis=-1, keepdims=True) + eps) * g


def _silu(x):
    h = 0.5 * x
    return h + h * jnp.tanh(h)


def _softplus(x):
    return jnp.maximum(x, 0.0) + jnp.log1p(jnp.exp(-jnp.abs(x)))


def _split2(x):
    hi = x.astype(BF16)
    lo = (x - hi.astype(F32)).astype(BF16)
    return hi, lo


def _split3(x):
    hi = x.astype(BF16)
    r = x - hi.astype(F32)
    mid = r.astype(BF16)
    lo = (r - mid.astype(F32)).astype(BF16)
    return hi, mid, lo


def _dot(a, b):
    return jnp.dot(a, b, preferred_element_type=F32)


def _dot_nt(a, b):
    return lax.dot_general(a, b, (((1,), (1,)), ((), ())), preferred_element_type=F32)


MXU_COLS = 256


def _attn_proj_kernel(x_ref, g_ref, w_ref, cos_ref, sin_ref, o_ref, *, rope_cols, query_cols):
    xn = _rms(x_ref[...], g_ref[...], NORM_EPS).astype(BF16)
    tm = xn.shape[0]
    cos = cos_ref[...]
    sin = sin_ref[...]
    lane = lax.broadcasted_iota(jnp.int32, (tm, LANES), 1)
    first_half = (lane & (HEAD_DIM // 2)) == 0
    for c in range(w_ref.shape[1] // MXU_COLS):
        lo = c * MXU_COLS
        acc = _dot(xn, w_ref[:, lo:lo + MXU_COLS])
        is_query = any(a <= lo < b for a, b in query_cols)
        if lo < rope_cols:
            for s in range(MXU_COLS // LANES):
                t = acc[:, s * LANES:(s + 1) * LANES]
                partner = jnp.where(first_half,
                                    pltpu.roll(t, LANES - HEAD_DIM // 2, 1),
                                    pltpu.roll(t, HEAD_DIM // 2, 1))
                r = t * cos + partner * sin
                if is_query:
                    r = r * QUERY_SCALE
                o_ref[:, lo + s * LANES:lo + (s + 1) * LANES] = r.astype(o_ref.dtype)
        else:
            if is_query:
                acc = acc * QUERY_SCALE
            o_ref[:, lo:lo + MXU_COLS] = acc.astype(o_ref.dtype)


def _attn_proj(x2d, g, w_bf, cos, sin, seq, tm):
    m, d = x2d.shape
    n = w_bf.shape[1]
    diff_w = DIFF_HEADS * 2 * HEAD_DIM
    sb_w = SB_HEADS * HEAD_DIM
    rope_cols = 2 * diff_w
    query_cols = ((0, diff_w), (3 * diff_w, 3 * diff_w + sb_w))
    pos_blocks = seq // tm
    return pl.pallas_call(
        functools.partial(_attn_proj_kernel, rope_cols=rope_cols, query_cols=query_cols),
        grid=(m // tm,),
        in_specs=[
            pl.BlockSpec((tm, d), lambda i: (i, 0)),
            pl.BlockSpec((1, d), lambda i: (0, 0)),
            pl.BlockSpec((d, n), lambda i: (0, 0)),
            pl.BlockSpec((tm, LANES), lambda i: (i % pos_blocks, 0)),
            pl.BlockSpec((tm, LANES), lambda i: (i % pos_blocks, 0)),
        ],
        out_specs=pl.BlockSpec((tm, n), lambda i: (i, 0)),
        out_shape=jax.ShapeDtypeStruct((m, n), BF16),
        compiler_params=pltpu.CompilerParams(
            dimension_semantics=("arbitrary",), vmem_limit_bytes=VMEM_LIMIT),
        name="attn_proj",
    )(x2d, g, w_bf, cos, sin)


def _ssd_proj_kernel(x_ref, g_ref, w_ref, wdt_ref, cw_ref, cb_ref, o_ref, dt_ref, tail_ref, *, z_cols, pos_blocks):
    xn = _rms(x_ref[...], g_ref[...], NORM_EPS).astype(BF16)
    dt_ref[...] = _dot(xn, wdt_ref[...])
    tm = xn.shape[0]
    row8 = lax.broadcasted_iota(jnp.int32, (SUBLANES, MXU_COLS), 0)

    @pl.when(pl.program_id(0) % pos_blocks == 0)
    def _():
        tail_ref[...] = jnp.zeros(tail_ref.shape, F32)

    n_chunks = w_ref.shape[1] // MXU_COLS
    acc_next = _dot(xn, w_ref[:, :MXU_COLS])
    for c in range(n_chunks):
        lo = c * MXU_COLS
        acc = acc_next
        if c + 1 < n_chunks:
            acc_next = _dot(xn, w_ref[:, lo + MXU_COLS:lo + 2 * MXU_COLS])
        if lo < z_cols:
            out = _silu(acc)
        else:
            cl = lo - z_cols
            tail = tail_ref[:, cl:cl + MXU_COLS]
            tail_ref[:, cl:cl + MXU_COLS] = acc[tm - SUBLANES:, :]
            conv = cb_ref[:, cl:cl + MXU_COLS] + cw_ref[SSD_CONV - 1:SSD_CONV, cl:cl + MXU_COLS] * acc
            for d in range(1, SSD_CONV):
                rolled = pltpu.roll(acc, d, 0)
                top = jnp.where(row8 < d, pltpu.roll(tail, d, 0), rolled[:SUBLANES])
                shifted = jnp.concatenate([top, rolled[SUBLANES:]], axis=0)
                conv = conv + cw_ref[SSD_CONV - 1 - d:SSD_CONV - d, cl:cl + MXU_COLS] * shifted
            out = _silu(conv)
        o_ref[:, lo:lo + MXU_COLS] = out.astype(o_ref.dtype)


def _ssd_proj(x2d, g, w_bf, wdt_bf, conv_w, conv_b, seq, tm):
    m, d = x2d.shape
    n = w_bf.shape[1]
    conv_dim = conv_w.shape[1]
    kern = functools.partial(_ssd_proj_kernel, z_cols=n - conv_dim, pos_blocks=seq // tm)
    return pl.pallas_call(
        kern,
        grid=(m // tm,),
        in_specs=[
            pl.BlockSpec((tm, d), lambda i: (i, 0)),
            pl.BlockSpec((1, d), lambda i: (0, 0)),
            pl.BlockSpec((d, n), lambda i: (0, 0)),
            pl.BlockSpec((d, LANES), lambda i: (0, 0)),
            pl.BlockSpec(conv_w.shape, lambda i: (0, 0)),
            pl.BlockSpec(conv_b.shape, lambda i: (0, 0)),
        ],
        out_specs=[
            pl.BlockSpec((tm, n), lambda i: (i, 0)),
            pl.BlockSpec((tm, LANES), lambda i: (i, 0)),
        ],
        out_shape=[jax.ShapeDtypeStruct((m, n), BF16), jax.ShapeDtypeStruct((m, LANES), F32)],
        scratch_shapes=[pltpu.VMEM((SUBLANES, conv_dim), F32)],
        compiler_params=pltpu.CompilerParams(
            dimension_semantics=("arbitrary",), vmem_limit_bytes=VMEM_LIMIT),
        name="ssd_proj",
    )(x2d, g, w_bf, wdt_bf, conv_w, conv_b)


def _diff_attn_kernel(lqk_ref, q_ref, k_ref, v_ref, subln_ref, o_ref, vt_ref, s0_ref, s1_ref, *, tq, lambda_init):
    qi = pl.program_id(2)
    seq = k_ref.shape[1]
    gq = MXU_COLS
    halves = tq // gq

    @pl.when(qi == 0)
    def _():
        for c in range(seq // tq):
            vt_ref[:LANES, c * tq:(c + 1) * tq] = v_ref[0, c * tq:(c + 1) * tq, :].T
        vt_ref[LANES:, :] = jnp.ones((vt_ref.shape[0] - LANES, seq), BF16)

    q = q_ref[0]
    lane = lax.broadcasted_iota(jnp.int32, (tq, LANES), 1)
    zero = jnp.zeros_like(q)
    qq = jnp.concatenate([jnp.where(lane < HEAD_DIM, q, zero), jnp.where(lane >= HEAD_DIM, q, zero)], axis=0)
    nq = 2 * tq
    key_i = lax.broadcasted_iota(jnp.int32, (gq, nq), 0)
    qry_i = lax.broadcasted_iota(jnp.int32, (gq, nq), 1) & (tq - 1)

    s_refs = (s0_ref, s1_ref)

    def scores(kb, slot):
        start = pl.multiple_of(kb * gq, gq)
        s_refs[slot][...] = _dot_nt(k_ref[0, pl.ds(start, gq), :], qq)

    def consume(state, kb, slot, diag=None):
        m, acc = state
        start = pl.multiple_of(kb * gq, gq)
        s = s_refs[slot][...]
        if diag is not None:
            s = jnp.where(key_i + diag * gq <= qry_i, s, -jnp.inf)
        m_new = jnp.maximum(m, jnp.max(s, axis=0, keepdims=True))
        p = jnp.exp2(s - m_new).astype(BF16)
        acc = jnp.exp2(m - m_new) * acc + _dot(vt_ref[:, pl.ds(start, gq)], p)
        return m_new, acc

    n_full = qi * halves

    def body(j, state):
        kb = j * halves
        scores(kb + 1, 1)
        state = consume(state, kb, 0)
        scores(kb + 2, 0)
        return consume(state, kb + 1, 1)

    scores(0, 0)
    init = (jnp.full((1, nq), -jnp.inf, F32), jnp.zeros((vt_ref.shape[0], nq), F32))
    state = lax.fori_loop(0, qi, body, init)
    scores(n_full + 1, 1)
    state = consume(state, n_full, 0, diag=0)
    state = consume(state, n_full + 1, 1, diag=1)

    lqk = lqk_ref[...]
    lam = (jnp.exp(jnp.sum(lqk[0:1] * lqk[1:2], axis=-1, keepdims=True))
           - jnp.exp(jnp.sum(lqk[2:3] * lqk[3:4], axis=-1, keepdims=True)) + lambda_init)
    _, acc = state
    o_t = acc[:LANES] / acc[LANES:LANES + 1]
    o_t = o_t[:, :tq] - lam * o_t[:, tq:]
    o = _rms(o_t.T, subln_ref[...], SUBLN_EPS) * (1.0 - lambda_init)
    o_ref[0] = o.astype(o_ref.dtype)


def _diff_attn(proj, lqk, subln, lambda_init, tq):
    b, s, _ = proj.shape
    h = DIFF_HEADS
    assert tq == 2 * MXU_COLS and s % tq == 0, (tq, s)
    kern = functools.partial(_diff_attn_kernel, tq=tq, lambda_init=lambda_init)
    return pl.pallas_call(
        kern,
        grid=(b, h, s // tq),
        in_specs=[
            pl.BlockSpec((4, HEAD_DIM), lambda bi, hi, qi: (0, 0)),
            pl.BlockSpec((1, tq, LANES), lambda bi, hi, qi: (bi, qi, hi)),
            pl.BlockSpec((1, s, LANES), lambda bi, hi, qi: (bi, 0, h + hi)),
            pl.BlockSpec((1, s, LANES), lambda bi, hi, qi: (bi, 0, 2 * h + hi)),
            pl.BlockSpec((1, LANES), lambda bi, hi, qi: (0, 0)),
        ],
        out_specs=pl.BlockSpec((1, tq, LANES), lambda bi, hi, qi: (bi, qi, hi)),
        out_shape=jax.ShapeDtypeStruct((b, s, h * LANES), BF16),
        scratch_shapes=[pltpu.VMEM((LANES + 2 * SUBLANES, s), BF16),
                        pltpu.VMEM((MXU_COLS, 2 * tq), F32), pltpu.VMEM((MXU_COLS, 2 * tq), F32)],
        compiler_params=pltpu.CompilerParams(
            dimension_semantics=("arbitrary", "arbitrary", "arbitrary"), vmem_limit_bytes=VMEM_LIMIT),
        name="diff_attn",
    )(lqk, proj, proj, proj, subln)


def _sb_attn_kernel(q_ref, k_ref, v_ref, o_ref, *, tq, tiles):
    first = pl.program_id(2) * tiles
    rows = 2 * tiles * tq
    lane = lax.broadcasted_iota(jnp.int32, (tq, LANES), 1)
    row = lax.broadcasted_iota(jnp.int32, (rows, tq), 0)
    col = lax.broadcasted_iota(jnp.int32, (rows, tq), 1)
    strict = col < (row & (tq - 1))
    suffix = jnp.where(lax.broadcasted_iota(jnp.int32, (tq, tq), 1) <= lax.broadcasted_iota(jnp.int32, (tq, tq), 0),
                       1.0, 0.0).astype(BF16)
    suffix2 = jnp.concatenate([suffix, suffix], axis=0)
    tile_of_row = lax.broadcasted_iota(jnp.int32, (rows, 1), 0) // (2 * tq)

    qqs = []
    for r in range(tiles):
        q = q_ref[0, r * tq:(r + 1) * tq, :]
        zero = jnp.zeros_like(q)
        qqs.append(jnp.concatenate([jnp.where(lane < HEAD_DIM, q, zero), jnp.where(lane >= HEAD_DIM, q, zero)], axis=0))

    def step(i, run, acc, masked):
        vs, zs = [], []
        for r in range(tiles):
            start = pl.multiple_of(jnp.maximum(first + r - i, 0) * tq, tq)
            zs.append(_dot_nt(qqs[r], k_ref[0, pl.ds(start, tq), :]))
            vs.append(v_ref[0, pl.ds(start, tq), :])
        z = jnp.concatenate(zs, axis=0)
        log_keep = -(jnp.maximum(z, 0.0) + jnp.log2(1.0 + jnp.exp2(-jnp.abs(z))))
        if masked:
            log_keep = jnp.where(strict, log_keep, 0.0)
        hi, lo = _split2(log_keep)
        csum = _dot(jnp.concatenate([hi, lo], axis=1), suffix2)
        run_in = run if masked else jnp.where(tile_of_row >= i - first, run, -1e30)
        a = jnp.exp2(z + csum + run_in)
        if masked:
            a = jnp.where(strict, a, 0.0)
        a = a.astype(BF16)
        pv = [_dot(a[r * 2 * tq:(r + 1) * 2 * tq], vs[r]) for r in range(tiles)]
        return run_in + csum[:, 0:1], acc + jnp.concatenate(pv, axis=0)

    run, acc = step(0, jnp.zeros((rows, 1), F32), jnp.zeros((rows, LANES), F32), True)
    run, acc = step(1, run, acc, False)

    def cond(st):
        i, run, _ = st
        return jnp.logical_and(first + tiles - 1 - i >= 0, jnp.max(run) > F32_EXP2_ZERO)

    def body(st):
        i, run, acc = st
        run, acc = step(i, run, acc, False)
        return i + 1, run, acc

    _, _, acc = lax.while_loop(cond, body, (jnp.int32(2), run, acc))
    for r in range(tiles):
        o_ref[0, r * tq:(r + 1) * tq, :] = jnp.where(
            lane < HEAD_DIM, acc[2 * r * tq:(2 * r + 1) * tq], acc[(2 * r + 1) * tq:(2 * r + 2) * tq]).astype(o_ref.dtype)


def _sb_attn(proj, tq, tiles):
    b, s, _ = proj.shape
    pairs = SB_HEADS * HEAD_DIM // LANES
    base = 3 * DIFF_HEADS * 2 * HEAD_DIM // LANES
    kern = functools.partial(_sb_attn_kernel, tq=tq, tiles=tiles)
    rows = tq * tiles
    return pl.pallas_call(
        kern,
        grid=(b, pairs, s // rows),
        in_specs=[
            pl.BlockSpec((1, rows, LANES), lambda bi, hi, qi: (bi, qi, base + hi)),
            pl.BlockSpec((1, s, LANES), lambda bi, hi, qi: (bi, 0, base + pairs + hi)),
            pl.BlockSpec((1, s, LANES), lambda bi, hi, qi: (bi, 0, base + 2 * pairs + hi)),
        ],
        out_specs=pl.BlockSpec((1, rows, LANES), lambda bi, hi, qi: (bi, qi, hi)),
        out_shape=jax.ShapeDtypeStruct((b, s, pairs * LANES), BF16),
        compiler_params=pltpu.CompilerParams(
            dimension_semantics=("arbitrary", "arbitrary", "arbitrary"), vmem_limit_bytes=VMEM_LIMIT),
        name="sb_attn",
    )(proj, proj, proj)


def _mix_ffn_kernel(*refs, n_y, sub, final):
    h_ref = refs[0]
    y_refs = refs[1:1 + n_y]
    wo_refs = refs[1 + n_y:1 + 2 * n_y]
    fg_ref, wg_ref, wu_ref, wd_ref = refs[1 + 2 * n_y:5 + 2 * n_y]
    fin_ref = refs[5 + 2 * n_y] if final else None
    o_ref, n_ref, acc_ref = refs[-3:]
    j = pl.program_id(1)

    @pl.when(j == 0)
    def _():
        h1 = h_ref[...]
        for y_ref, wo_ref in zip(y_refs, wo_refs):
            h1 = h1 + _dot(y_ref[...], wo_ref[...])
        n_ref[...] = _rms(h1, fg_ref[...], NORM_EPS).astype(BF16)
        acc_ref[...] = h1

    n = n_ref[...]
    fc = wg_ref.shape[1]
    acc = acc_ref[...]
    for lo in range(0, fc, sub):
        hi = min(lo + sub, fc)
        g = _dot(n, wg_ref[:, lo:hi])
        u = _dot(n, wu_ref[:, lo:hi])
        a = (_silu(g) * u).astype(BF16)
        acc = acc + _dot(a, wd_ref[lo:hi, :])
    acc_ref[...] = acc

    @pl.when(j == pl.num_programs(1) - 1)
    def _():
        out = acc_ref[...]
        if final:
            out = _rms(out, fin_ref[...], NORM_EPS)
        o_ref[...] = out


def _ffn_tiles(f):
    half = f // 2
    fc = half if f % 2 == 0 and half % LANES == 0 else f
    return fc, 2 * LANES


def _mix_ffn(h2d, ys, wos, fg, wg, wu, wd, fin, tm, fc, sub):
    m, d = h2d.shape
    f = wg.shape[1]
    final = fin is not None
    kern = functools.partial(_mix_ffn_kernel, n_y=len(ys), sub=sub, final=final)
    const = lambda shape: pl.BlockSpec(shape, lambda i, j: (0,) * len(shape))
    in_specs = [pl.BlockSpec((tm, d), lambda i, j: (i, 0))]
    in_specs += [pl.BlockSpec((tm, y.shape[1]), lambda i, j: (i, 0)) for y in ys]
    in_specs += [const(w.shape) for w in wos]
    in_specs += [const(fg.shape),
                 pl.BlockSpec((d, fc), lambda i, j: (0, j)),
                 pl.BlockSpec((d, fc), lambda i, j: (0, j)),
                 pl.BlockSpec((fc, d), lambda i, j: (j, 0))]
    args = [h2d, *ys, *wos, fg, wg, wu, wd]
    if final:
        in_specs.append(const(fin.shape))
        args.append(fin)
    return pl.pallas_call(
        kern,
        grid=(m // tm, f // fc),
        in_specs=in_specs,
        out_specs=pl.BlockSpec((tm, d), lambda i, j: (i, 0)),
        out_shape=jax.ShapeDtypeStruct((m, d), F32),
        scratch_shapes=[pltpu.VMEM((tm, d), BF16), pltpu.VMEM((tm, d), F32)],
        compiler_params=pltpu.CompilerParams(
            dimension_semantics=("arbitrary", "arbitrary"), vmem_limit_bytes=VMEM_LIMIT),
        name="mix_ffn",
    )(*args)


def _ssd_kernel(zx_ref, dt_ref, dtb_ref, alog_ref, dsk_ref, gn_ref, exp_ref, o_ref, state, *, chunk):
    L = chunk
    d_inner = SSD_HEADS * SSD_HEAD_DIM
    gw = d_inner // SSD_GROUPS
    hpg = SSD_HEADS // SSD_GROUPS
    x_off, b_off, c_off = d_inner, 2 * d_inner, 2 * d_inner + SSD_GROUPS * SSD_STATE

    @pl.when(pl.program_id(1) == 0)
    def _():
        state[...] = jnp.zeros(state.shape, F32)

    lane = lax.broadcasted_iota(jnp.int32, (1, LANES), 1)
    a = jnp.where(lane < SSD_HEADS, -jnp.exp(alog_ref[...]), 0.0)
    dt = _softplus(dt_ref[0] + dtb_ref[...])
    row = lax.broadcasted_iota(jnp.int32, (L, L), 0)
    col = lax.broadcasted_iota(jnp.int32, (L, L), 1)
    lower = row >= col
    tril = jnp.where(lower, 1.0, 0.0).astype(BF16)
    d_hi, d_mid, d_lo = _split3(dt * a)
    acum = _dot(tril, d_hi) + _dot(tril, d_mid) + _dot(tril, d_lo)
    last = acum[L - 1:L, :]
    ea = jnp.exp(acum)
    w = jnp.exp(last - acum) * dt
    acum_t = acum.T
    dt_t = dt.T
    w_t = w.T
    c_hi, c_mid, c_lo = _split3(jnp.broadcast_to(jnp.exp(last), (SUBLANES, LANES)))
    e = exp_ref[...]
    cd_full = (_dot(c_hi, e) + _dot(c_mid, e) + _dot(c_lo, e))[0:1, :]

    glane = lax.broadcasted_iota(jnp.int32, (1, gw), 1)
    for g in range(SSD_GROUPS):
        xg_bf = zx_ref[0, :, x_off + g * gw:x_off + (g + 1) * gw]
        xg = xg_bf.astype(F32)
        bg_bf = zx_ref[0, :, b_off + g * SSD_STATE:b_off + (g + 1) * SSD_STATE]
        cg_bf = zx_ref[0, :, c_off + g * SSD_STATE:c_off + (g + 1) * SSD_STATE]
        cg = cg_bf.astype(F32)
        cb = _dot_nt(cg_bf, bg_bf)
        bg_t = bg_bf.astype(F32).T
        prev = state[g]
        prev_bf = prev.astype(BF16)
        y_lhs, y_rhs, st_lhs, st_rhs = [], [], [], []
        for j in range(hpg):
            h = g * hpg + j
            in_head = jnp.logical_and(glane >= j * SSD_HEAD_DIM, glane < (j + 1) * SSD_HEAD_DIM)
            x_h = jnp.where(in_head, xg_bf, jnp.zeros_like(xg_bf))
            prev_h = jnp.where(in_head, prev_bf, jnp.zeros_like(prev_bf))
            seg = acum[:, h:h + 1] - acum_t[h:h + 1, :]
            decay = jnp.exp(jnp.where(lower, seg, -jnp.inf))
            y_lhs += [(cb * decay * dt_t[h:h + 1, :]).astype(BF16), (cg * ea[:, h:h + 1]).astype(BF16)]
            y_rhs += [x_h, prev_h]
            st_lhs.append((bg_t * w_t[h:h + 1, :]).astype(BF16))
            st_rhs.append(x_h)
        y = dsk_ref[:, g * gw:(g + 1) * gw] * xg + _dot(jnp.concatenate(y_lhs, axis=1), jnp.concatenate(y_rhs, axis=0))
        state[g] = prev * cd_full[:, g * gw:(g + 1) * gw] + _dot(
            jnp.concatenate(st_lhs, axis=1), jnp.concatenate(st_rhs, axis=0))
        gated = y * zx_ref[0, :, g * gw:(g + 1) * gw].astype(F32)
        o_ref[0, :, g * gw:(g + 1) * gw] = _rms(gated, gn_ref[:, g * gw:(g + 1) * gw], SUBLN_EPS).astype(o_ref.dtype)


def _ssd_scan(zx, dt_raw, dt_bias, a_log, d_full, gnorm, expand, chunk):
    b, s, width = zx.shape
    d_inner = SSD_HEADS * SSD_HEAD_DIM
    kern = functools.partial(_ssd_kernel, chunk=chunk)
    const = lambda shape: pl.BlockSpec(shape, lambda bi, ci: (0,) * len(shape))
    return pl.pallas_call(
        kern,
        grid=(b, s // chunk),
        in_specs=[
            pl.BlockSpec((1, chunk, width), lambda bi, ci: (bi, ci, 0)),
            pl.BlockSpec((1, chunk, LANES), lambda bi, ci: (bi, ci, 0)),
            const(dt_bias.shape), const(a_log.shape),
            const(d_full.shape), const(gnorm.shape), const(expand.shape),
        ],
        out_specs=pl.BlockSpec((1, chunk, d_inner), lambda bi, ci: (bi, ci, 0)),
        out_shape=jax.ShapeDtypeStruct((b, s, d_inner), BF16),
        scratch_shapes=[pltpu.VMEM((SSD_GROUPS, SSD_STATE, d_inner // SSD_GROUPS), F32)],
        compiler_params=pltpu.CompilerParams(
            dimension_semantics=("arbitrary", "arbitrary"), vmem_limit_bytes=VMEM_LIMIT),
        name="ssd_scan",
    )(zx, dt_raw, dt_bias, a_log, d_full, gnorm, expand)


def _rope_tables(seq):
    half = HEAD_DIM // 2
    lane = jnp.arange(LANES)
    inv_freq = ROPE_THETA ** (-(lane % half).astype(F32) / half)
    ang = jnp.arange(seq, dtype=F32)[:, None] * inv_freq[None, :]
    sign = jnp.where((lane % HEAD_DIM) < half, -1.0, 1.0)
    return jnp.cos(ang), jnp.sin(ang) * sign[None, :]


def _pad_lanes(v):
    return jnp.pad(v, (0, LANES - v.shape[0]))[None, :]


def _attn_layer(h2d, b, s, norm_w, w_in, lq1, lk1, lq2, lk2, subln, w_out, fg, wg, wu, wd, fin, lambda_init,
                tm, tq_diff, tq_sb, sb_tiles):
    cos, sin = _rope_tables(s)
    proj = _attn_proj(h2d, norm_w[None, :], w_in.astype(BF16), cos, sin, s, tm)
    proj = proj.reshape(b, s, -1)
    lqk = jnp.stack([lq1, lk1, lq2, lk2])
    o_diff = _diff_attn(proj, lqk, subln[None, :], lambda_init, tq_diff)
    o_sb = _sb_attn(proj, tq_sb, sb_tiles)
    nd = o_diff.shape[-1]
    w_out_bf = w_out.astype(BF16)
    return _mix_ffn(h2d, [o_diff.reshape(b * s, -1), o_sb.reshape(b * s, -1)], [w_out_bf[:nd], w_out_bf[nd:]],
                    fg[None, :], wg.astype(BF16), wu.astype(BF16), wd.astype(BF16), fin, tm, *_ffn_tiles(wg.shape[1]))


def _ssd_layer(h2d, b, s, norm_w, w_in, conv_w, conv_b, dt_bias, a_log, d_skip, gnorm, w_out,
               fg, wg, wu, wd, fin, tm, chunk):
    d_inner = SSD_HEADS * SSD_HEAD_DIM
    main = w_in.shape[1] - SSD_HEADS
    w_bf = w_in.astype(BF16)
    w_dt = jnp.pad(w_bf[:, main:], ((0, 0), (0, LANES - SSD_HEADS)))
    zx, dt_raw = _ssd_proj(h2d, norm_w[None, :], w_bf[:, :main], w_dt, conv_w, conv_b[None, :], s, tm)
    expand = (jnp.arange(LANES)[:, None] == (jnp.arange(d_inner)[None, :] // SSD_HEAD_DIM)).astype(BF16)
    y = _ssd_scan(zx.reshape(b, s, main), dt_raw.reshape(b, s, LANES),
                  _pad_lanes(dt_bias), _pad_lanes(a_log), jnp.repeat(d_skip, SSD_HEAD_DIM)[None, :],
                  gnorm[None, :], expand, chunk)
    return _mix_ffn(h2d, [y.reshape(b * s, d_inner)], [w_out.astype(BF16)],
                    fg[None, :], wg.astype(BF16), wu.astype(BF16), wd.astype(BF16), fin, tm, *_ffn_tiles(wg.shape[1]))


def kernel(x, attn_norm, attn_w_in, diff_lq1, diff_lk1, diff_lq2, diff_lk2, diff_subln, attn_w_out, ssd_norm, ssd_w_in, ssd_conv_w, ssd_conv_b, ssd_dt_bias, ssd_a_log, ssd_d, ssd_gnorm, ssd_w_out, ffn_norm, ffn_w_gate, ffn_w_up, ffn_w_down, final_norm):
    b, s, d = x.shape
    depth = ffn_norm.shape[0]
    tm = min(512, s)
    h = x.reshape(b * s, d)
    for layer in range(depth):
        i = layer // 2
        fin = final_norm[None, :] if layer == depth - 1 else None
        ffn = (ffn_norm[layer], ffn_w_gate[layer], ffn_w_up[layer], ffn_w_down[layer], fin)
        if layer % 2 == 0:
            lambda_init = 0.8 - 0.6 * math.exp(-0.3 * layer)
            h = _attn_layer(h, b, s, attn_norm[i], attn_w_in[i], diff_lq1[i], diff_lk1[i], diff_lq2[i],
                            diff_lk2[i], diff_subln[i], attn_w_out[i], *ffn, lambda_init,
                            tm=tm, tq_diff=min(512, s), tq_sb=min(256, s), sb_tiles=4 if s % 1024 == 0 else 1)
        else:
            h = _ssd_layer(h, b, s, ssd_norm[i], ssd_w_in[i], ssd_conv_w[i], ssd_conv_b[i], ssd_dt_bias[i],
                           ssd_a_log[i], ssd_d[i], ssd_gnorm[i], ssd_w_out[i], *ffn,
                           tm=tm, chunk=128)
    return h.reshape(b, s, d)
```

```python
import functools
import math

import jax
import jax.numpy as jnp
from jax import lax
from jax.experimental import pallas as pl
from jax.experimental.pallas import tpu as pltpu

F32 = jnp.float32
BF16 = jnp.bfloat16

HEAD_DIM = 64
DIFF_HEADS = 4
SB_HEADS = 8
ROPE_THETA = 10000.0
NORM_EPS = 1e-6
SUBLN_EPS = 1e-5
SSD_HEADS = 32
SSD_GROUPS = 8
SSD_STATE = 128
SSD_CONV = 4
SSD_HEAD_DIM = 64
LANES = 128
SUBLANES = 8
VMEM_LIMIT = 56 * 1024 * 1024

LOG2E = 1.4426950408889634
QUERY_SCALE = LOG2E * HEAD_DIM ** -0.5
F32_EXP2_ZERO = -151.0


def _rms(x, g, eps):
    return x * lax.rsqrt(jnp.mean(x * x, axis=-1, keepdims=True) + eps) * g


def _silu(x):
    h = 0.5 * x
    return h + h * jnp.tanh(h)


def _softplus(x):
    return jnp.maximum(x, 0.0) + jnp.log1p(jnp.exp(-jnp.abs(x)))


def _split2(x):
    hi = x.astype(BF16)
    lo = (x - hi.astype(F32)).astype(BF16)
    return hi, lo


def _split3(x):
    hi = x.astype(BF16)
    r = x - hi.astype(F32)
    mid = r.astype(BF16)
    lo = (r - mid.astype(F32)).astype(BF16)
    return hi, mid, lo


def _dot(a, b):
    return jnp.dot(a, b, preferred_element_type=F32)


def _dot_nt(a, b):
    return lax.dot_general(a, b, (((1,), (1,)), ((), ())), preferred_element_type=F32)


MXU_COLS = 256


def _attn_proj_kernel(x_ref, g_ref, w_ref, cos_ref, sin_ref, o_ref, *, rope_cols, query_cols):
    xn = _rms(x_ref[...], g_ref[...], NORM_EPS).astype(BF16)
    tm = xn.shape[0]
    cos = cos_ref[...]
    sin = sin_ref[...]
    lane = lax.broadcasted_iota(jnp.int32, (tm, LANES), 1)
    first_half = (lane & (HEAD_DIM // 2)) == 0
    for c in range(w_ref.shape[1] // MXU_COLS):
        lo = c * MXU_COLS
        acc = _dot(xn, w_ref[:, lo:lo + MXU_COLS])
        is_query = any(a <= lo < b for a, b in query_cols)
        if lo < rope_cols:
            for s in range(MXU_COLS // LANES):
                t = acc[:, s * LANES:(s + 1) * LANES]
                partner = jnp.where(first_half,
                                    pltpu.roll(t, LANES - HEAD_DIM // 2, 1),
                                    pltpu.roll(t, HEAD_DIM // 2, 1))
                r = t * cos + partner * sin
                if is_query:
                    r = r * QUERY_SCALE
                o_ref[:, lo + s * LANES:lo + (s + 1) * LANES] = r.astype(o_ref.dtype)
        else:
            if is_query:
                acc = acc * QUERY_SCALE
            o_ref[:, lo:lo + MXU_COLS] = acc.astype(o_ref.dtype)


def _attn_proj(x2d, g, w_bf, cos, sin, seq, tm):
    m, d = x2d.shape
    n = w_bf.shape[1]
    diff_w = DIFF_HEADS * 2 * HEAD_DIM
    sb_w = SB_HEADS * HEAD_DIM
    rope_cols = 2 * diff_w
    query_cols = ((0, diff_w), (3 * diff_w, 3 * diff_w + sb_w))
    pos_blocks = seq // tm
    return pl.pallas_call(
        functools.partial(_attn_proj_kernel, rope_cols=rope_cols, query_cols=query_cols),
        grid=(m // tm,),
        in_specs=[
            pl.BlockSpec((tm, d), lambda i: (i, 0)),
            pl.BlockSpec((1, d), lambda i: (0, 0)),
            pl.BlockSpec((d, n), lambda i: (0, 0)),
            pl.BlockSpec((tm, LANES), lambda i: (i % pos_blocks, 0)),
            pl.BlockSpec((tm, LANES), lambda i: (i % pos_blocks, 0)),
        ],
        out_specs=pl.BlockSpec((tm, n), lambda i: (i, 0)),
        out_shape=jax.ShapeDtypeStruct((m, n), BF16),
        compiler_params=pltpu.CompilerParams(
            dimension_semantics=("arbitrary",), vmem_limit_bytes=VMEM_LIMIT),
        name="attn_proj",
    )(x2d, g, w_bf, cos, sin)


def _ssd_proj_kernel(x_ref, g_ref, w_ref, wdt_ref, cw_ref, cb_ref, o_ref, dt_ref, tail_ref, *, z_cols, pos_blocks):
    xn = _rms(x_ref[...], g_ref[...], NORM_EPS).astype(BF16)
    dt_ref[...] = _dot(xn, wdt_ref[...])
    tm = xn.shape[0]
    row8 = lax.broadcasted_iota(jnp.int32, (SUBLANES, MXU_COLS), 0)

    @pl.when(pl.program_id(0) % pos_blocks == 0)
    def _():
        tail_ref[...] = jnp.zeros(tail_ref.shape, F32)

    n_chunks = o_ref.shape[1] // MXU_COLS
    acc_next = _dot(xn, w_ref[:, :MXU_COLS])
    for c in range(n_chunks):
        lo = c * MXU_COLS
        acc = acc_next
        if c + 1 < n_chunks:
            acc_next = _dot(xn, w_ref[:, lo + MXU_COLS:lo + 2 * MXU_COLS])
        if lo < z_cols:
            out = _silu(acc)
        else:
            cl = lo - z_cols
            tail = tail_ref[:, cl:cl + MXU_COLS]
            tail_ref[:, cl:cl + MXU_COLS] = acc[tm - SUBLANES:, :]
            conv = cb_ref[:, cl:cl + MXU_COLS] + cw_ref[SSD_CONV - 1:SSD_CONV, cl:cl + MXU_COLS] * acc
            for d in range(1, SSD_CONV):
                rolled = pltpu.roll(acc, d, 0)
                top = jnp.where(row8 < d, pltpu.roll(tail, d, 0), rolled[:SUBLANES])
                shifted = jnp.concatenate([top, rolled[SUBLANES:]], axis=0)
                conv = conv + cw_ref[SSD_CONV - 1 - d:SSD_CONV - d, cl:cl + MXU_COLS] * shifted
            out = _silu(conv)
        o_ref[:, lo:lo + MXU_COLS] = out.astype(o_ref.dtype)


def _ssd_proj(x2d, g, w_bf, wdt_bf, conv_w, conv_b, seq, tm):
    m, d = x2d.shape
    conv_dim = conv_w.shape[1]
    n = SSD_HEADS * SSD_HEAD_DIM + conv_dim
    assert n % MXU_COLS == 0 and n <= w_bf.shape[1]
    kern = functools.partial(_ssd_proj_kernel, z_cols=n - conv_dim, pos_blocks=seq // tm)
    return pl.pallas_call(
        kern,
        grid=(m // tm,),
        in_specs=[
            pl.BlockSpec((tm, d), lambda i: (i, 0)),
            pl.BlockSpec((1, d), lambda i: (0, 0)),
            pl.BlockSpec(w_bf.shape, lambda i: (0, 0)),
            pl.BlockSpec((d, LANES), lambda i: (0, 0)),
            pl.BlockSpec(conv_w.shape, lambda i: (0, 0)),
            pl.BlockSpec(conv_b.shape, lambda i: (0, 0)),
        ],
        out_specs=[
            pl.BlockSpec((tm, n), lambda i: (i, 0)),
            pl.BlockSpec((tm, LANES), lambda i: (i, 0)),
        ],
        out_shape=[jax.ShapeDtypeStruct((m, n), BF16), jax.ShapeDtypeStruct((m, LANES), F32)],
        scratch_shapes=[pltpu.VMEM((SUBLANES, conv_dim), F32)],
        compiler_params=pltpu.CompilerParams(
            dimension_semantics=("arbitrary",), vmem_limit_bytes=VMEM_LIMIT),
        name="ssd_proj",
    )(x2d, g, w_bf, wdt_bf, conv_w, conv_b)


def _diff_attn_kernel(lqk_ref, q_ref, k_ref, v_ref, subln_ref, o_ref, vt_ref, s0_ref, s1_ref, *, tq, lambda_init):
    qi = pl.program_id(2)
    seq = k_ref.shape[1]
    gq = MXU_COLS
    halves = tq // gq

    @pl.when(qi == 0)
    def _():
        for c in range(seq // tq):
            vt_ref[:LANES, c * tq:(c + 1) * tq] = v_ref[0, c * tq:(c + 1) * tq, :].T
        vt_ref[LANES:, :] = jnp.ones((vt_ref.shape[0] - LANES, seq), BF16)

    q = q_ref[0]
    lane = lax.broadcasted_iota(jnp.int32, (tq, LANES), 1)
    zero = jnp.zeros_like(q)
    qq = jnp.concatenate([jnp.where(lane < HEAD_DIM, q, zero), jnp.where(lane >= HEAD_DIM, q, zero)], axis=0)
    nq = 2 * tq
    key_i = lax.broadcasted_iota(jnp.int32, (gq, nq), 0)
    qry_i = lax.broadcasted_iota(jnp.int32, (gq, nq), 1) & (tq - 1)

    s_refs = (s0_ref, s1_ref)

    def scores(kb, slot):
        start = pl.multiple_of(kb * gq, gq)
        s_refs[slot][...] = _dot_nt(k_ref[0, pl.ds(start, gq), :], qq)

    def consume(state, kb, slot, diag=None):
        m, acc = state
        start = pl.multiple_of(kb * gq, gq)
        s = s_refs[slot][...]
        if diag is not None:
            s = jnp.where(key_i + diag * gq <= qry_i, s, -jnp.inf)
        m_new = jnp.maximum(m, jnp.max(s, axis=0, keepdims=True))
        p = jnp.exp2(s - m_new).astype(BF16)
        acc = jnp.exp2(m - m_new) * acc + _dot(vt_ref[:, pl.ds(start, gq)], p)
        return m_new, acc

    n_full = qi * halves

    def body(j, state):
        kb = j * halves
        scores(kb + 1, 1)
        state = consume(state, kb, 0)
        scores(kb + 2, 0)
        return consume(state, kb + 1, 1)

    scores(0, 0)
    init = (jnp.full((1, nq), -jnp.inf, F32), jnp.zeros((vt_ref.shape[0], nq), F32))
    state = lax.fori_loop(0, qi, body, init)
    scores(n_full + 1, 1)
    state = consume(state, n_full, 0, diag=0)
    state = consume(state, n_full + 1, 1, diag=1)

    lqk = lqk_ref[...]
    lam = (jnp.exp(jnp.sum(lqk[0:1] * lqk[1:2], axis=-1, keepdims=True))
           - jnp.exp(jnp.sum(lqk[2:3] * lqk[3:4], axis=-1, keepdims=True)) + lambda_init)
    _, acc = state
    o_t = acc[:LANES] / acc[LANES:LANES + 1]
    o_t = o_t[:, :tq] - lam * o_t[:, tq:]
    o = _rms(o_t.T, subln_ref[...], SUBLN_EPS) * (1.0 - lambda_init)
    o_ref[0] = o.astype(o_ref.dtype)


def _diff_attn(proj, lqk, subln, lambda_init, tq):
    b, s, _ = proj.shape
    h = DIFF_HEADS
    assert tq == 2 * MXU_COLS and s % tq == 0, (tq, s)
    kern = functools.partial(_diff_attn_kernel, tq=tq, lambda_init=lambda_init)
    return pl.pallas_call(
        kern,
        grid=(b, h, s // tq),
        in_specs=[
            pl.BlockSpec((4, HEAD_DIM), lambda bi, hi, qi: (0, 0)),
            pl.BlockSpec((1, tq, LANES), lambda bi, hi, qi: (bi, qi, hi)),
            pl.BlockSpec((1, s, LANES), lambda bi, hi, qi: (bi, 0, h + hi)),
            pl.BlockSpec((1, s, LANES), lambda bi, hi, qi: (bi, 0, 2 * h + hi)),
            pl.BlockSpec((1, LANES), lambda bi, hi, qi: (0, 0)),
        ],
        out_specs=pl.BlockSpec((1, tq, LANES), lambda bi, hi, qi: (bi, qi, hi)),
        out_shape=jax.ShapeDtypeStruct((b, s, h * LANES), BF16),
        scratch_shapes=[pltpu.VMEM((LANES + 2 * SUBLANES, s), BF16),
                        pltpu.VMEM((MXU_COLS, 2 * tq), F32), pltpu.VMEM((MXU_COLS, 2 * tq), F32)],
        compiler_params=pltpu.CompilerParams(
            dimension_semantics=("arbitrary", "arbitrary", "arbitrary"), vmem_limit_bytes=VMEM_LIMIT),
        name="diff_attn",
    )(lqk, proj, proj, proj, subln)


def _sb_attn_kernel(q_ref, k_ref, v_ref, o_ref, *, tq, tiles):
    first = pl.program_id(2) * tiles
    rows = 2 * tiles * tq
    lane = lax.broadcasted_iota(jnp.int32, (tq, LANES), 1)
    row = lax.broadcasted_iota(jnp.int32, (rows, tq), 0)
    col = lax.broadcasted_iota(jnp.int32, (rows, tq), 1)
    strict = col < (row & (tq - 1))
    suffix = jnp.where(lax.broadcasted_iota(jnp.int32, (tq, tq), 1) <= lax.broadcasted_iota(jnp.int32, (tq, tq), 0),
                       1.0, 0.0).astype(BF16)
    suffix2 = jnp.concatenate([suffix, suffix], axis=0)
    tile_of_row = lax.broadcasted_iota(jnp.int32, (rows, 1), 0) // (2 * tq)

    qqs = []
    for r in range(tiles):
        q = q_ref[0, r * tq:(r + 1) * tq, :]
        zero = jnp.zeros_like(q)
        qqs.append(jnp.concatenate([jnp.where(lane < HEAD_DIM, q, zero), jnp.where(lane >= HEAD_DIM, q, zero)], axis=0))

    def step(i, run, acc, masked):
        vs, zs = [], []
        for r in range(tiles):
            start = pl.multiple_of(jnp.maximum(first + r - i, 0) * tq, tq)
            zs.append(_dot_nt(qqs[r], k_ref[0, pl.ds(start, tq), :]))
            vs.append(v_ref[0, pl.ds(start, tq), :])
        z = jnp.concatenate(zs, axis=0)
        log_keep = -(jnp.maximum(z, 0.0) + jnp.log2(1.0 + jnp.exp2(-jnp.abs(z))))
        if masked:
            log_keep = jnp.where(strict, log_keep, 0.0)
        hi, lo = _split2(log_keep)
        csum = _dot(jnp.concatenate([hi, lo], axis=1), suffix2)
        run_in = run if masked else jnp.where(tile_of_row >= i - first, run, -1e30)
        a = jnp.exp2(z + csum + run_in)
        if masked:
            a = jnp.where(strict, a, 0.0)
        a = a.astype(BF16)
        pv = [_dot(a[r * 2 * tq:(r + 1) * 2 * tq], vs[r]) for r in range(tiles)]
        return run_in + csum[:, 0:1], acc + jnp.concatenate(pv, axis=0)

    run, acc = step(0, jnp.zeros((rows, 1), F32), jnp.zeros((rows, LANES), F32), True)
    run, acc = step(1, run, acc, False)

    def cond(st):
        i, run, _ = st
        return jnp.logical_and(first + tiles - 1 - i >= 0, jnp.max(run) > F32_EXP2_ZERO)

    def body(st):
        i, run, acc = st
        run, acc = step(i, run, acc, False)
        return i + 1, run, acc

    _, _, acc = lax.while_loop(cond, body, (jnp.int32(2), run, acc))
    for r in range(tiles):
        o_ref[0, r * tq:(r + 1) * tq, :] = jnp.where(
            lane < HEAD_DIM, acc[2 * r * tq:(2 * r + 1) * tq], acc[(2 * r + 1) * tq:(2 * r + 2) * tq]).astype(o_ref.dtype)


def _sb_attn(proj, tq, tiles):
    b, s, _ = proj.shape
    pairs = SB_HEADS * HEAD_DIM // LANES
    base = 3 * DIFF_HEADS * 2 * HEAD_DIM // LANES
    kern = functools.partial(_sb_attn_kernel, tq=tq, tiles=tiles)
    rows = tq * tiles
    return pl.pallas_call(
        kern,
        grid=(b, pairs, s // rows),
        in_specs=[
            pl.BlockSpec((1, rows, LANES), lambda bi, hi, qi: (bi, qi, base + hi)),
            pl.BlockSpec((1, s, LANES), lambda bi, hi, qi: (bi, 0, base + pairs + hi)),
            pl.BlockSpec((1, s, LANES), lambda bi, hi, qi: (bi, 0, base + 2 * pairs + hi)),
        ],
        out_specs=pl.BlockSpec((1, rows, LANES), lambda bi, hi, qi: (bi, qi, hi)),
        out_shape=jax.ShapeDtypeStruct((b, s, pairs * LANES), BF16),
        compiler_params=pltpu.CompilerParams(
            dimension_semantics=("arbitrary", "arbitrary", "arbitrary"), vmem_limit_bytes=VMEM_LIMIT),
        name="sb_attn",
    )(proj, proj, proj)


def _mix_ffn_kernel(*refs, n_y, sub, final, ffn_width, n_steps):
    h_ref = refs[0]
    y_refs = refs[1:1 + n_y]
    wo_refs = refs[1 + n_y:1 + 2 * n_y]
    fg_ref, wg_ref, wu_ref, wd_ref = refs[1 + 2 * n_y:5 + 2 * n_y]
    fin_ref = refs[5 + 2 * n_y] if final else None
    o_ref, n_ref, acc_ref = refs[-3:]
    j = pl.program_id(1)

    @pl.when(j == 0)
    def _():
        h1 = h_ref[...]
        for y_ref, wo_ref in zip(y_refs, wo_refs):
            h1 = h1 + _dot(y_ref[...], wo_ref[...])
        n_ref[...] = _rms(h1, fg_ref[...], NORM_EPS).astype(BF16)
        acc_ref[...] = h1

    n = n_ref[...]
    fc = wg_ref.shape[1]
    last_j = pl.num_programs(1) - 1

    def chunk(lo):
        g = _dot(n, wg_ref[:, lo:lo + sub])
        u = _dot(n, wu_ref[:, lo:lo + sub])
        return _dot((_silu(g) * u).astype(BF16), wd_ref[lo:lo + sub, :])

    chunks = list(range(0, fc, sub))
    beyond = [lo for lo in chunks if (n_steps - 1) * fc + lo + sub > ffn_width]
    for lo in beyond:
        @pl.when(j < last_j)
        def _(lo=lo):
            acc_ref[...] += chunk(lo)

    acc = acc_ref[...]
    for lo in chunks:
        if lo not in beyond:
            acc = acc + chunk(lo)
    acc_ref[...] = acc

    @pl.when(j == last_j)
    def _():
        out = acc_ref[...]
        if final:
            out = _rms(out, fin_ref[...], NORM_EPS)
        o_ref[...] = out


def _ffn_tiles(f):
    sub = MXU_COLS
    assert f % sub == 0, f
    n_sub = f // sub
    return -(-n_sub // 2) * sub, sub


def _mix_ffn(h2d, ys, wos, fg, wg, wu, wd, layer, fin, tm, fc, sub):
    m, d = h2d.shape
    f = wg.shape[2]
    n_steps = -(-f // fc)
    final = fin is not None
    kern = functools.partial(_mix_ffn_kernel, n_y=len(ys), sub=sub, final=final, ffn_width=f, n_steps=n_steps)
    const = lambda shape: pl.BlockSpec(shape, lambda i, j: (0,) * len(shape))
    in_specs = [pl.BlockSpec((tm, d), lambda i, j: (i, 0))]
    in_specs += [pl.BlockSpec((tm, y.shape[1]), lambda i, j: (i, 0)) for y in ys]
    in_specs += [pl.BlockSpec((y.shape[1], d), lambda i, j, k=k: (k, 0)) for k, y in enumerate(ys)]
    in_specs += [const(fg.shape),
                 pl.BlockSpec((None, d, fc), lambda i, j: (layer, 0, j)),
                 pl.BlockSpec((None, d, fc), lambda i, j: (layer, 0, j)),
                 pl.BlockSpec((None, fc, d), lambda i, j: (layer, j, 0))]
    args = [h2d, *ys, *wos, fg, wg, wu, wd]
    if final:
        in_specs.append(const(fin.shape))
        args.append(fin)
    return pl.pallas_call(
        kern,
        grid=(m // tm, n_steps),
        in_specs=in_specs,
        out_specs=pl.BlockSpec((tm, d), lambda i, j: (i, 0)),
        out_shape=jax.ShapeDtypeStruct((m, d), F32),
        scratch_shapes=[pltpu.VMEM((tm, d), BF16), pltpu.VMEM((tm, d), F32)],
        compiler_params=pltpu.CompilerParams(
            dimension_semantics=("arbitrary", "arbitrary"), vmem_limit_bytes=VMEM_LIMIT),
        name="mix_ffn",
    )(*args)


def _ssd_kernel(zx_ref, dt_ref, dtb_ref, alog_ref, dsk_ref, gn_ref, exp_ref, o_ref, state, *, chunk):
    L = chunk
    d_inner = SSD_HEADS * SSD_HEAD_DIM
    gw = d_inner // SSD_GROUPS
    hpg = SSD_HEADS // SSD_GROUPS
    x_off, b_off, c_off = d_inner, 2 * d_inner, 2 * d_inner + SSD_GROUPS * SSD_STATE

    @pl.when(pl.program_id(1) == 0)
    def _():
        state[...] = jnp.zeros(state.shape, F32)

    lane = lax.broadcasted_iota(jnp.int32, (1, LANES), 1)
    a = jnp.where(lane < SSD_HEADS, -jnp.exp(alog_ref[...]), 0.0)
    dt = _softplus(dt_ref[0] + dtb_ref[...])
    row = lax.broadcasted_iota(jnp.int32, (L, L), 0)
    col = lax.broadcasted_iota(jnp.int32, (L, L), 1)
    lower = row >= col
    tril = jnp.where(lower, 1.0, 0.0).astype(BF16)
    d_hi, d_mid, d_lo = _split3(dt * a)
    acum = _dot(tril, d_hi) + _dot(tril, d_mid) + _dot(tril, d_lo)
    last = acum[L - 1:L, :]
    ea = jnp.exp(acum)
    w = jnp.exp(last - acum) * dt
    acum_t = acum.T
    dt_t = dt.T
    w_t = w.T
    c_hi, c_mid, c_lo = _split3(jnp.broadcast_to(jnp.exp(last), (SUBLANES, LANES)))
    e = exp_ref[...]
    cd_full = (_dot(c_hi, e) + _dot(c_mid, e) + _dot(c_lo, e))[0:1, :]

    glane = lax.broadcasted_iota(jnp.int32, (1, gw), 1)
    for g in range(SSD_GROUPS):
        xg_bf = zx_ref[0, :, x_off + g * gw:x_off + (g + 1) * gw]
        xg = xg_bf.astype(F32)
        bg_bf = zx_ref[0, :, b_off + g * SSD_STATE:b_off + (g + 1) * SSD_STATE]
        cg_bf = zx_ref[0, :, c_off + g * SSD_STATE:c_off + (g + 1) * SSD_STATE]
        cg = cg_bf.astype(F32)
        cb = _dot_nt(cg_bf, bg_bf)
        bg_t = bg_bf.astype(F32).T
        prev = state[g]
        prev_bf = prev.astype(BF16)
        y_lhs, y_rhs, st_lhs, st_rhs = [], [], [], []
        for j in range(hpg):
            h = g * hpg + j
            in_head = jnp.logical_and(glane >= j * SSD_HEAD_DIM, glane < (j + 1) * SSD_HEAD_DIM)
            x_h = jnp.where(in_head, xg_bf, jnp.zeros_like(xg_bf))
            prev_h = jnp.where(in_head, prev_bf, jnp.zeros_like(prev_bf))
            seg = acum[:, h:h + 1] - acum_t[h:h + 1, :]
            decay = jnp.exp(jnp.where(lower, seg, -jnp.inf))
            y_lhs += [(cb * decay * dt_t[h:h + 1, :]).astype(BF16), (cg * ea[:, h:h + 1]).astype(BF16)]
            y_rhs += [x_h, prev_h]
            st_lhs.append((bg_t * w_t[h:h + 1, :]).astype(BF16))
            st_rhs.append(x_h)
        y = dsk_ref[:, g * gw:(g + 1) * gw] * xg + _dot(jnp.concatenate(y_lhs, axis=1), jnp.concatenate(y_rhs, axis=0))
        state[g] = prev * cd_full[:, g * gw:(g + 1) * gw] + _dot(
            jnp.concatenate(st_lhs, axis=1), jnp.concatenate(st_rhs, axis=0))
        gated = y * zx_ref[0, :, g * gw:(g + 1) * gw].astype(F32)
        o_ref[0, :, g * gw:(g + 1) * gw] = _rms(gated, gn_ref[:, g * gw:(g + 1) * gw], SUBLN_EPS).astype(o_ref.dtype)


def _ssd_scan(zx, dt_raw, dt_bias, a_log, d_full, gnorm, expand, chunk):
    b, s, width = zx.shape
    d_inner = SSD_HEADS * SSD_HEAD_DIM
    kern = functools.partial(_ssd_kernel, chunk=chunk)
    const = lambda shape: pl.BlockSpec(shape, lambda bi, ci: (0,) * len(shape))
    return pl.pallas_call(
        kern,
        grid=(b, s // chunk),
        in_specs=[
            pl.BlockSpec((1, chunk, width), lambda bi, ci: (bi, ci, 0)),
            pl.BlockSpec((1, chunk, LANES), lambda bi, ci: (bi, ci, 0)),
            const(dt_bias.shape), const(a_log.shape),
            const(d_full.shape), const(gnorm.shape), const(expand.shape),
        ],
        out_specs=pl.BlockSpec((1, chunk, d_inner), lambda bi, ci: (bi, ci, 0)),
        out_shape=jax.ShapeDtypeStruct((b, s, d_inner), BF16),
        scratch_shapes=[pltpu.VMEM((SSD_GROUPS, SSD_STATE, d_inner // SSD_GROUPS), F32)],
        compiler_params=pltpu.CompilerParams(
            dimension_semantics=("arbitrary", "arbitrary"), vmem_limit_bytes=VMEM_LIMIT),
        name="ssd_scan",
    )(zx, dt_raw, dt_bias, a_log, d_full, gnorm, expand)


def _rope_tables(seq):
    half = HEAD_DIM // 2
    inv_freq = ROPE_THETA ** (-jnp.arange(half, dtype=F32) / half)
    ang = jnp.arange(seq, dtype=F32)[:, None] * inv_freq[None, :]
    cos, sin = jnp.cos(ang), jnp.sin(ang)
    reps = LANES // HEAD_DIM
    return jnp.tile(jnp.concatenate([cos, cos], axis=1), (1, reps)), jnp.tile(jnp.concatenate([-sin, sin], axis=1), (1, reps))


def _pad_lanes(v):
    return jnp.pad(v, (0, LANES - v.shape[0]))[None, :]


def _attn_layer(h2d, b, s, norm_w, w_in, lq1, lk1, lq2, lk2, subln, w_out, ffn, lambda_init,
                tm, tq_diff, tq_sb, sb_tiles):
    cos, sin = _rope_tables(s)
    proj = _attn_proj(h2d, norm_w[None, :], w_in.astype(BF16), cos, sin, s, tm)
    proj = proj.reshape(b, s, -1)
    lqk = jnp.stack([lq1, lk1, lq2, lk2])
    o_diff = _diff_attn(proj, lqk, subln[None, :], lambda_init, tq_diff)
    o_sb = _sb_attn(proj, tq_sb, sb_tiles)
    assert o_diff.shape[-1] == o_sb.shape[-1]
    w_out_bf = w_out.astype(BF16)
    return _mix_ffn(h2d, [o_diff.reshape(b * s, -1), o_sb.reshape(b * s, -1)], [w_out_bf, w_out_bf], *ffn, tm,
                    *_ffn_tiles(ffn[1].shape[2]))


def _ssd_layer(h2d, b, s, norm_w, w_in, conv_w, conv_b, dt_bias, a_log, d_skip, gnorm, w_out, ffn, tm, chunk):
    d_inner = SSD_HEADS * SSD_HEAD_DIM
    main = w_in.shape[1] - SSD_HEADS
    w_bf = w_in.astype(BF16)
    w_dt = jnp.pad(w_bf[:, main:], ((0, 0), (0, LANES - SSD_HEADS)))
    zx, dt_raw = _ssd_proj(h2d, norm_w[None, :], w_bf, w_dt, conv_w, conv_b[None, :], s, tm)
    expand = (jnp.arange(LANES)[:, None] == (jnp.arange(d_inner)[None, :] // SSD_HEAD_DIM)).astype(BF16)
    y = _ssd_scan(zx.reshape(b, s, main), dt_raw.reshape(b, s, LANES),
                  _pad_lanes(dt_bias), _pad_lanes(a_log), jnp.repeat(d_skip, SSD_HEAD_DIM)[None, :],
                  gnorm[None, :], expand, chunk)
    return _mix_ffn(h2d, [y.reshape(b * s, d_inner)], [w_out.astype(BF16)], *ffn, tm, *_ffn_tiles(ffn[1].shape[2]))


def kernel(x, attn_norm, attn_w_in, diff_lq1, diff_lk1, diff_lq2, diff_lk2, diff_subln, attn_w_out, ssd_norm, ssd_w_in, ssd_conv_w, ssd_conv_b, ssd_dt_bias, ssd_a_log, ssd_d, ssd_gnorm, ssd_w_out, ffn_norm, ffn_w_gate, ffn_w_up, ffn_w_down, final_norm):
    b, s, d = x.shape
    depth = ffn_norm.shape[0]
    tm = min(512, s)
    h = x.reshape(b * s, d)
    wg, wu, wd = ffn_w_gate.astype(BF16), ffn_w_up.astype(BF16), ffn_w_down.astype(BF16)
    for layer in range(depth):
        i = layer // 2
        fin = final_norm[None, :] if layer == depth - 1 else None
        ffn = (ffn_norm[layer][None, :], wg, wu, wd, layer, fin)
        if layer % 2 == 0:
            lambda_init = 0.8 - 0.6 * math.exp(-0.3 * layer)
            h = _attn_layer(h, b, s, attn_norm[i], attn_w_in[i], diff_lq1[i], diff_lk1[i], diff_lq2[i],
                            diff_lk2[i], diff_subln[i], attn_w_out[i], ffn, lambda_init,
                            tm=tm, tq_diff=min(512, s), tq_sb=min(256, s), sb_tiles=4 if s % 1024 == 0 else 1)
        else:
            h = _ssd_layer(h, b, s, ssd_norm[i], ssd_w_in[i], ssd_conv_w[i], ssd_conv_b[i], ssd_dt_bias[i],
                           ssd_a_log[i], ssd_d[i], ssd_gnorm[i], ssd_w_out[i], ffn,
                           tm=tm, chunk=128)
    return h.reshape(b, s, d)
```

```python
import functools
import math

import jax
import jax.numpy as jnp
from jax import lax
from jax.experimental import pallas as pl
from jax.experimental.pallas import tpu as pltpu

F32 = jnp.float32
BF16 = jnp.bfloat16

HEAD_DIM = 64
DIFF_HEADS = 4
SB_HEADS = 8
ROPE_THETA = 10000.0
NORM_EPS = 1e-6
SUBLN_EPS = 1e-5
SSD_HEADS = 32
SSD_GROUPS = 8
SSD_STATE = 128
SSD_CONV = 4
SSD_HEAD_DIM = 64
LANES = 128
SUBLANES = 8
VMEM_LIMIT = 56 * 1024 * 1024

LOG2E = 1.4426950408889634
QUERY_SCALE = LOG2E * HEAD_DIM ** -0.5
F32_EXP2_ZERO = -151.0


def _rms(x, g, eps):
    return x * lax.rsqrt(jnp.mean(x * x, axis=-1, keepdims=True) + eps) * g


def _silu(x):
    h = 0.5 * x
    return h + h * jnp.tanh(h)


def _softplus(x):
    return jnp.maximum(x, 0.0) + jnp.log1p(jnp.exp(-jnp.abs(x)))


def _split2(x):
    hi = x.astype(BF16)
    lo = (x - hi.astype(F32)).astype(BF16)
    return hi, lo


def _split3(x):
    hi = x.astype(BF16)
    r = x - hi.astype(F32)
    mid = r.astype(BF16)
    lo = (r - mid.astype(F32)).astype(BF16)
    return hi, mid, lo


def _dot(a, b):
    return jnp.dot(a, b, preferred_element_type=F32)


def _dot_nt(a, b):
    return lax.dot_general(a, b, (((1,), (1,)), ((), ())), preferred_element_type=F32)


MXU_COLS = 256


def _attn_proj_kernel(x_ref, g_ref, w_ref, cos_ref, sin_ref, o_ref, *, rope_cols, query_cols):
    xn = _rms(x_ref[...], g_ref[...], NORM_EPS).astype(BF16)
    tm = xn.shape[0]
    cos = cos_ref[...]
    sin = sin_ref[...]
    lane = lax.broadcasted_iota(jnp.int32, (tm, LANES), 1)
    first_half = (lane & (HEAD_DIM // 2)) == 0
    for c in range(w_ref.shape[1] // MXU_COLS):
        lo = c * MXU_COLS
        acc = _dot(xn, w_ref[:, lo:lo + MXU_COLS])
        is_query = any(a <= lo < b for a, b in query_cols)
        if lo < rope_cols:
            for s in range(MXU_COLS // LANES):
                t = acc[:, s * LANES:(s + 1) * LANES]
                partner = jnp.where(first_half,
                                    pltpu.roll(t, LANES - HEAD_DIM // 2, 1),
                                    pltpu.roll(t, HEAD_DIM // 2, 1))
                r = t * cos + partner * sin
                if is_query:
                    r = r * QUERY_SCALE
                o_ref[:, lo + s * LANES:lo + (s + 1) * LANES] = r.astype(o_ref.dtype)
        else:
            if is_query:
                acc = acc * QUERY_SCALE
            o_ref[:, lo:lo + MXU_COLS] = acc.astype(o_ref.dtype)


def _attn_proj(x2d, g, w_bf, cos, sin, seq, tm):
    m, d = x2d.shape
    n = w_bf.shape[1]
    diff_w = DIFF_HEADS * 2 * HEAD_DIM
    sb_w = SB_HEADS * HEAD_DIM
    rope_cols = 2 * diff_w
    query_cols = ((0, diff_w), (3 * diff_w, 3 * diff_w + sb_w))
    pos_blocks = seq // tm
    return pl.pallas_call(
        functools.partial(_attn_proj_kernel, rope_cols=rope_cols, query_cols=query_cols),
        grid=(m // tm,),
        in_specs=[
            pl.BlockSpec((tm, d), lambda i: (i, 0)),
            pl.BlockSpec((1, d), lambda i: (0, 0)),
            pl.BlockSpec((d, n), lambda i: (0, 0)),
            pl.BlockSpec((tm, LANES), lambda i: (i % pos_blocks, 0)),
            pl.BlockSpec((tm, LANES), lambda i: (i % pos_blocks, 0)),
        ],
        out_specs=pl.BlockSpec((tm, n), lambda i: (i, 0)),
        out_shape=jax.ShapeDtypeStruct((m, n), BF16),
        compiler_params=pltpu.CompilerParams(
            dimension_semantics=("arbitrary",), vmem_limit_bytes=VMEM_LIMIT),
        name="attn_proj",
    )(x2d, g, w_bf, cos, sin)


def _ssd_proj_kernel(x_ref, g_ref, w_ref, wdt_ref, cw_ref, cb_ref, o_ref, dt_ref, tail_ref, *, z_cols, pos_blocks):
    xn = _rms(x_ref[...], g_ref[...], NORM_EPS).astype(BF16)
    dt_ref[...] = _dot(xn, wdt_ref[...])
    tm = xn.shape[0]
    row8 = lax.broadcasted_iota(jnp.int32, (SUBLANES, MXU_COLS), 0)

    @pl.when(pl.program_id(0) % pos_blocks == 0)
    def _():
        tail_ref[...] = jnp.zeros(tail_ref.shape, F32)

    n_chunks = o_ref.shape[1] // MXU_COLS
    acc_next = _dot(xn, w_ref[:, :MXU_COLS])
    for c in range(n_chunks):
        lo = c * MXU_COLS
        acc = acc_next
        if c + 1 < n_chunks:
            acc_next = _dot(xn, w_ref[:, lo + MXU_COLS:lo + 2 * MXU_COLS])
        if lo < z_cols:
            out = _silu(acc)
        else:
            cl = lo - z_cols
            tail = tail_ref[:, cl:cl + MXU_COLS]
            tail_ref[:, cl:cl + MXU_COLS] = acc[tm - SUBLANES:, :]
            conv = cb_ref[:, cl:cl + MXU_COLS] + cw_ref[SSD_CONV - 1:SSD_CONV, cl:cl + MXU_COLS] * acc
            for d in range(1, SSD_CONV):
                rolled = pltpu.roll(acc, d, 0)
                top = jnp.where(row8 < d, pltpu.roll(tail, d, 0), rolled[:SUBLANES])
                shifted = jnp.concatenate([top, rolled[SUBLANES:]], axis=0)
                conv = conv + cw_ref[SSD_CONV - 1 - d:SSD_CONV - d, cl:cl + MXU_COLS] * shifted
            out = _silu(conv)
        o_ref[:, lo:lo + MXU_COLS] = out.astype(o_ref.dtype)


def _ssd_proj(x2d, g, w_bf, wdt_bf, conv_w, conv_b, seq, tm):
    m, d = x2d.shape
    conv_dim = conv_w.shape[1]
    n = SSD_HEADS * SSD_HEAD_DIM + conv_dim
    assert n % MXU_COLS == 0 and n <= w_bf.shape[1]
    kern = functools.partial(_ssd_proj_kernel, z_cols=n - conv_dim, pos_blocks=seq // tm)
    return pl.pallas_call(
        kern,
        grid=(m // tm,),
        in_specs=[
            pl.BlockSpec((tm, d), lambda i: (i, 0)),
            pl.BlockSpec((1, d), lambda i: (0, 0)),
            pl.BlockSpec(w_bf.shape, lambda i: (0, 0)),
            pl.BlockSpec((d, LANES), lambda i: (0, 0)),
            pl.BlockSpec(conv_w.shape, lambda i: (0, 0)),
            pl.BlockSpec(conv_b.shape, lambda i: (0, 0)),
        ],
        out_specs=[
            pl.BlockSpec((tm, n), lambda i: (i, 0)),
            pl.BlockSpec((tm, LANES), lambda i: (i, 0)),
        ],
        out_shape=[jax.ShapeDtypeStruct((m, n), BF16), jax.ShapeDtypeStruct((m, LANES), F32)],
        scratch_shapes=[pltpu.VMEM((SUBLANES, conv_dim), F32)],
        compiler_params=pltpu.CompilerParams(
            dimension_semantics=("arbitrary",), vmem_limit_bytes=VMEM_LIMIT),
        name="ssd_proj",
    )(x2d, g, w_bf, wdt_bf, conv_w, conv_b)


def _diff_attn_kernel(lqk_ref, q_ref, k_ref, v_ref, subln_ref, o_ref, vt_ref, s0_ref, s1_ref, m_ref, acc_ref, *,
                      tq, lambda_init):
    qi = pl.program_id(2)
    seq = k_ref.shape[1]
    gq = MXU_COLS
    halves = tq // gq

    @pl.when(qi == 0)
    def _():
        for c in range(seq // tq):
            vt_ref[:LANES, c * tq:(c + 1) * tq] = v_ref[0, c * tq:(c + 1) * tq, :].T
        vt_ref[LANES:, :] = jnp.ones((vt_ref.shape[0] - LANES, seq), BF16)

    q = q_ref[0]
    lane = lax.broadcasted_iota(jnp.int32, (tq, LANES), 1)
    zero = jnp.zeros_like(q)
    qq = jnp.concatenate([jnp.where(lane < HEAD_DIM, q, zero), jnp.where(lane >= HEAD_DIM, q, zero)], axis=0)
    nq = 2 * tq
    key_i = lax.broadcasted_iota(jnp.int32, (gq, nq), 0)
    qry_i = lax.broadcasted_iota(jnp.int32, (gq, nq), 1) & (tq - 1)

    s_refs = (s0_ref, s1_ref)

    def scores(kb, slot):
        start = pl.multiple_of(kb * gq, gq)
        s_refs[slot][...] = _dot_nt(k_ref[0, pl.ds(start, gq), :], qq)

    def consume(kb, slot, diag=None):
        start = pl.multiple_of(kb * gq, gq)
        s = s_refs[slot][...]
        if diag is not None:
            s = jnp.where(key_i + diag * gq <= qry_i, s, -jnp.inf)
        m = m_ref[...]
        m_new = jnp.maximum(m, jnp.max(s, axis=0, keepdims=True))
        p = jnp.exp2(s - m_new).astype(BF16)
        acc_ref[...] = jnp.exp2(m - m_new) * acc_ref[...] + _dot(vt_ref[:, pl.ds(start, gq)], p)
        m_ref[...] = m_new

    def pair(kb):
        scores(kb + 1, 1)
        consume(kb, 0)
        scores(kb + 2, 0)
        consume(kb + 1, 1)

    def pairs(first_pair, count):
        for c in range(count):
            pair(2 * (first_pair + c))

    unroll = 4

    def body(j, carry):
        pairs(unroll * j, unroll)
        return carry

    n_full = qi * halves
    m_ref[...] = jnp.full(m_ref.shape, -jnp.inf, F32)
    acc_ref[...] = jnp.zeros(acc_ref.shape, F32)
    scores(0, 0)
    lax.fori_loop(0, qi // unroll, body, 0)
    done = (qi // unroll) * unroll
    for bit in (2, 1):
        @pl.when((qi & bit) != 0)
        def _(bit=bit, done=done):
            pairs(done, bit)
        done = done + (qi & bit)

    scores(n_full + 1, 1)
    consume(n_full, 0, diag=0)
    consume(n_full + 1, 1, diag=1)

    lqk = lqk_ref[...]
    lam = (jnp.exp(jnp.sum(lqk[0:1] * lqk[1:2], axis=-1, keepdims=True))
           - jnp.exp(jnp.sum(lqk[2:3] * lqk[3:4], axis=-1, keepdims=True)) + lambda_init)
    acc = acc_ref[...]
    o_t = acc[:LANES] / acc[LANES:LANES + 1]
    o_t = o_t[:, :tq] - lam * o_t[:, tq:]
    o = _rms(o_t.T, subln_ref[...], SUBLN_EPS) * (1.0 - lambda_init)
    o_ref[0] = o.astype(o_ref.dtype)


def _diff_attn(proj, lqk, subln, lambda_init, tq):
    b, s, _ = proj.shape
    h = DIFF_HEADS
    assert tq == 2 * MXU_COLS and s % tq == 0, (tq, s)
    kern = functools.partial(_diff_attn_kernel, tq=tq, lambda_init=lambda_init)
    return pl.pallas_call(
        kern,
        grid=(b, h, s // tq),
        in_specs=[
            pl.BlockSpec((4, HEAD_DIM), lambda bi, hi, qi: (0, 0)),
            pl.BlockSpec((1, tq, LANES), lambda bi, hi, qi: (bi, qi, hi)),
            pl.BlockSpec((1, s, LANES), lambda bi, hi, qi: (bi, 0, h + hi)),
            pl.BlockSpec((1, s, LANES), lambda bi, hi, qi: (bi, 0, 2 * h + hi)),
            pl.BlockSpec((1, LANES), lambda bi, hi, qi: (0, 0)),
        ],
        out_specs=pl.BlockSpec((1, tq, LANES), lambda bi, hi, qi: (bi, qi, hi)),
        out_shape=jax.ShapeDtypeStruct((b, s, h * LANES), BF16),
        scratch_shapes=[pltpu.VMEM((LANES + 2 * SUBLANES, s), BF16),
                        pltpu.VMEM((MXU_COLS, 2 * tq), F32), pltpu.VMEM((MXU_COLS, 2 * tq), F32),
                        pltpu.VMEM((1, 2 * tq), F32), pltpu.VMEM((LANES + 2 * SUBLANES, 2 * tq), F32)],
        compiler_params=pltpu.CompilerParams(
            dimension_semantics=("arbitrary", "arbitrary", "arbitrary"), vmem_limit_bytes=VMEM_LIMIT),
        name="diff_attn",
    )(lqk, proj, proj, proj, subln)


def _sb_attn_kernel(q_ref, k_ref, v_ref, o_ref, *, tq, tiles):
    first = pl.program_id(2) * tiles
    rows = 2 * tiles * tq
    lane = lax.broadcasted_iota(jnp.int32, (tq, LANES), 1)
    row = lax.broadcasted_iota(jnp.int32, (rows, tq), 0)
    col = lax.broadcasted_iota(jnp.int32, (rows, tq), 1)
    strict = col < (row & (tq - 1))
    suffix = jnp.where(lax.broadcasted_iota(jnp.int32, (tq, tq), 1) <= lax.broadcasted_iota(jnp.int32, (tq, tq), 0),
                       1.0, 0.0).astype(BF16)
    suffix2 = jnp.concatenate([suffix, suffix], axis=0)
    tile_of_row = lax.broadcasted_iota(jnp.int32, (rows, 1), 0) // (2 * tq)

    qqs = []
    for r in range(tiles):
        q = q_ref[0, r * tq:(r + 1) * tq, :]
        zero = jnp.zeros_like(q)
        qqs.append(jnp.concatenate([jnp.where(lane < HEAD_DIM, q, zero), jnp.where(lane >= HEAD_DIM, q, zero)], axis=0))

    def step(i, run, acc, masked):
        vs, zs = [], []
        for r in range(tiles):
            start = pl.multiple_of(jnp.maximum(first + r - i, 0) * tq, tq)
            zs.append(_dot_nt(qqs[r], k_ref[0, pl.ds(start, tq), :]))
            vs.append(v_ref[0, pl.ds(start, tq), :])
        z = jnp.concatenate(zs, axis=0)
        log_keep = -(jnp.maximum(z, 0.0) + jnp.log2(1.0 + jnp.exp2(-jnp.abs(z))))
        if masked:
            log_keep = jnp.where(strict, log_keep, 0.0)
        hi, lo = _split2(log_keep)
        csum = _dot(jnp.concatenate([hi, lo], axis=1), suffix2)
        run_in = run if masked else jnp.where(tile_of_row >= i - first, run, -1e30)
        a = jnp.exp2(z + csum + run_in)
        if masked:
            a = jnp.where(strict, a, 0.0)
        a = a.astype(BF16)
        pv = [_dot(a[r * 2 * tq:(r + 1) * 2 * tq], vs[r]) for r in range(tiles)]
        return run_in + csum[:, 0:1], acc + jnp.concatenate(pv, axis=0)

    run, acc = step(0, jnp.zeros((rows, 1), F32), jnp.zeros((rows, LANES), F32), True)
    run, acc = step(1, run, acc, False)

    def cond(st):
        i, run, _ = st
        return jnp.logical_and(first + tiles - 1 - i >= 0, jnp.max(run) > F32_EXP2_ZERO)

    def body(st):
        i, run, acc = st
        run, acc = step(i, run, acc, False)
        return i + 1, run, acc

    _, _, acc = lax.while_loop(cond, body, (jnp.int32(2), run, acc))
    for r in range(tiles):
        o_ref[0, r * tq:(r + 1) * tq, :] = jnp.where(
            lane < HEAD_DIM, acc[2 * r * tq:(2 * r + 1) * tq], acc[(2 * r + 1) * tq:(2 * r + 2) * tq]).astype(o_ref.dtype)


def _sb_attn(proj, tq, tiles):
    b, s, _ = proj.shape
    pairs = SB_HEADS * HEAD_DIM // LANES
    base = 3 * DIFF_HEADS * 2 * HEAD_DIM // LANES
    kern = functools.partial(_sb_attn_kernel, tq=tq, tiles=tiles)
    rows = tq * tiles
    return pl.pallas_call(
        kern,
        grid=(b, pairs, s // rows),
        in_specs=[
            pl.BlockSpec((1, rows, LANES), lambda bi, hi, qi: (bi, qi, base + hi)),
            pl.BlockSpec((1, s, LANES), lambda bi, hi, qi: (bi, 0, base + pairs + hi)),
            pl.BlockSpec((1, s, LANES), lambda bi, hi, qi: (bi, 0, base + 2 * pairs + hi)),
        ],
        out_specs=pl.BlockSpec((1, rows, LANES), lambda bi, hi, qi: (bi, qi, hi)),
        out_shape=jax.ShapeDtypeStruct((b, s, pairs * LANES), BF16),
        compiler_params=pltpu.CompilerParams(
            dimension_semantics=("arbitrary", "arbitrary", "arbitrary"), vmem_limit_bytes=VMEM_LIMIT),
        name="sb_attn",
    )(proj, proj, proj)


def _mix_ffn_kernel(*refs, n_y, sub, final, ffn_width, n_steps):
    h_ref = refs[0]
    y_refs = refs[1:1 + n_y]
    wo_refs = refs[1 + n_y:1 + 2 * n_y]
    fg_ref, wg_ref, wu_ref, wd_ref = refs[1 + 2 * n_y:5 + 2 * n_y]
    fin_ref = refs[5 + 2 * n_y] if final else None
    o_ref, n_ref, acc_ref = refs[-3:]
    j = pl.program_id(1)

    @pl.when(j == 0)
    def _():
        h1 = h_ref[...]
        for y_ref, wo_ref in zip(y_refs, wo_refs):
            h1 = h1 + _dot(y_ref[...], wo_ref[...])
        n_ref[...] = _rms(h1, fg_ref[...], NORM_EPS).astype(BF16)
        acc_ref[...] = h1

    n = n_ref[...]
    fc = wg_ref.shape[1]
    last_j = pl.num_programs(1) - 1

    def chunk(lo):
        g = _dot(n, wg_ref[:, lo:lo + sub])
        u = _dot(n, wu_ref[:, lo:lo + sub])
        return _dot((_silu(g) * u).astype(BF16), wd_ref[lo:lo + sub, :])

    chunks = list(range(0, fc, sub))
    beyond = [lo for lo in chunks if (n_steps - 1) * fc + lo + sub > ffn_width]
    for lo in beyond:
        @pl.when(j < last_j)
        def _(lo=lo):
            acc_ref[...] += chunk(lo)

    acc = acc_ref[...]
    for lo in chunks:
        if lo not in beyond:
            acc = acc + chunk(lo)
    acc_ref[...] = acc

    @pl.when(j == last_j)
    def _():
        out = acc_ref[...]
        if final:
            out = _rms(out, fin_ref[...], NORM_EPS)
        o_ref[...] = out


def _ffn_tiles(f):
    sub = MXU_COLS
    assert f % sub == 0, f
    n_sub = f // sub
    return -(-n_sub // 2) * sub, sub


def _mix_ffn(h2d, ys, wos, fg, wg, wu, wd, layer, fin, tm, fc, sub):
    m, d = h2d.shape
    f = wg.shape[2]
    n_steps = -(-f // fc)
    final = fin is not None
    kern = functools.partial(_mix_ffn_kernel, n_y=len(ys), sub=sub, final=final, ffn_width=f, n_steps=n_steps)
    const = lambda shape: pl.BlockSpec(shape, lambda i, j: (0,) * len(shape))
    in_specs = [pl.BlockSpec((tm, d), lambda i, j: (i, 0))]
    in_specs += [pl.BlockSpec((tm, y.shape[1]), lambda i, j: (i, 0)) for y in ys]
    in_specs += [pl.BlockSpec((y.shape[1], d), lambda i, j, k=k: (k, 0)) for k, y in enumerate(ys)]
    in_specs += [const(fg.shape),
                 pl.BlockSpec((None, d, fc), lambda i, j: (layer, 0, j)),
                 pl.BlockSpec((None, d, fc), lambda i, j: (layer, 0, j)),
                 pl.BlockSpec((None, fc, d), lambda i, j: (layer, j, 0))]
    args = [h2d, *ys, *wos, fg, wg, wu, wd]
    if final:
        in_specs.append(const(fin.shape))
        args.append(fin)
    return pl.pallas_call(
        kern,
        grid=(m // tm, n_steps),
        in_specs=in_specs,
        out_specs=pl.BlockSpec((tm, d), lambda i, j: (i, 0)),
        out_shape=jax.ShapeDtypeStruct((m, d), F32),
        scratch_shapes=[pltpu.VMEM((tm, d), BF16), pltpu.VMEM((tm, d), F32)],
        compiler_params=pltpu.CompilerParams(
            dimension_semantics=("arbitrary", "arbitrary"), vmem_limit_bytes=VMEM_LIMIT),
        name="mix_ffn",
    )(*args)


def _ssd_kernel(zx_ref, dt_ref, dtb_ref, alog_ref, dsk_ref, gn_ref, exp_ref, o_ref, state, *, chunk):
    L = chunk
    gw = SSD_HEADS * SSD_HEAD_DIM // SSD_GROUPS

    @pl.when(pl.program_id(1) == 0)
    def _():
        state[...] = jnp.zeros(state.shape, F32)

    lane = lax.broadcasted_iota(jnp.int32, (1, LANES), 1)
    a = jnp.where(lane < SSD_HEADS, -jnp.exp(alog_ref[...]), 0.0)
    row = lax.broadcasted_iota(jnp.int32, (L, L), 0)
    col = lax.broadcasted_iota(jnp.int32, (L, L), 1)
    lower = row >= col
    tril = jnp.where(lower, 1.0, 0.0).astype(BF16)
    e = exp_ref[...]
    glane = lax.broadcasted_iota(jnp.int32, (1, gw), 1)
    for r0 in range(0, zx_ref.shape[1], L):
        _ssd_chunk(zx_ref, dt_ref, dtb_ref, dsk_ref, gn_ref, o_ref, state, r0, L, a, lower, tril, e, glane)


def _ssd_chunk(zx_ref, dt_ref, dtb_ref, dsk_ref, gn_ref, o_ref, state, r0, L, a, lower, tril, e, glane):
    d_inner = SSD_HEADS * SSD_HEAD_DIM
    gw = d_inner // SSD_GROUPS
    hpg = SSD_HEADS // SSD_GROUPS
    x_off, b_off, c_off = d_inner, 2 * d_inner, 2 * d_inner + SSD_GROUPS * SSD_STATE
    rows = slice(r0, r0 + L)
    dt = _softplus(dt_ref[0, rows, :] + dtb_ref[...])
    d_hi, d_mid, d_lo = _split3(dt * a)
    acum = _dot(tril, d_hi) + _dot(tril, d_mid) + _dot(tril, d_lo)
    last = acum[L - 1:L, :]
    ea = jnp.exp(acum)
    w = jnp.exp(last - acum) * dt
    acum_t = acum.T
    dt_t = dt.T
    w_t = w.T
    c_hi, c_mid, c_lo = _split3(jnp.broadcast_to(jnp.exp(last), (SUBLANES, LANES)))
    cd_full = (_dot(c_hi, e) + _dot(c_mid, e) + _dot(c_lo, e))[0:1, :]

    for g in range(SSD_GROUPS):
        xg_bf = zx_ref[0, rows, x_off + g * gw:x_off + (g + 1) * gw]
        xg = xg_bf.astype(F32)
        bg_bf = zx_ref[0, rows, b_off + g * SSD_STATE:b_off + (g + 1) * SSD_STATE]
        cg_bf = zx_ref[0, rows, c_off + g * SSD_STATE:c_off + (g + 1) * SSD_STATE]
        cg = cg_bf.astype(F32)
        cb = _dot_nt(cg_bf, bg_bf)
        bg_t = bg_bf.astype(F32).T
        prev = state[g]
        prev_bf = prev.astype(BF16)
        y_lhs, y_rhs, st_lhs, st_rhs = [], [], [], []
        for j in range(hpg):
            h = g * hpg + j
            in_head = jnp.logical_and(glane >= j * SSD_HEAD_DIM, glane < (j + 1) * SSD_HEAD_DIM)
            x_h = jnp.where(in_head, xg_bf, jnp.zeros_like(xg_bf))
            prev_h = jnp.where(in_head, prev_bf, jnp.zeros_like(prev_bf))
            seg = acum[:, h:h + 1] - acum_t[h:h + 1, :]
            decay = jnp.exp(jnp.where(lower, seg, -jnp.inf))
            y_lhs += [(cb * decay * dt_t[h:h + 1, :]).astype(BF16), (cg * ea[:, h:h + 1]).astype(BF16)]
            y_rhs += [x_h, prev_h]
            st_lhs.append((bg_t * w_t[h:h + 1, :]).astype(BF16))
            st_rhs.append(x_h)
        y = dsk_ref[:, g * gw:(g + 1) * gw] * xg + _dot(jnp.concatenate(y_lhs, axis=1), jnp.concatenate(y_rhs, axis=0))
        state[g] = prev * cd_full[:, g * gw:(g + 1) * gw] + _dot(
            jnp.concatenate(st_lhs, axis=1), jnp.concatenate(st_rhs, axis=0))
        gated = y * zx_ref[0, rows, g * gw:(g + 1) * gw].astype(F32)
        o_ref[0, rows, g * gw:(g + 1) * gw] = _rms(
            gated, gn_ref[:, g * gw:(g + 1) * gw], SUBLN_EPS).astype(o_ref.dtype)


def _ssd_scan(zx, dt_raw, dt_bias, a_log, d_full, gnorm, expand, chunk, chunks_per_step):
    b, s, width = zx.shape
    d_inner = SSD_HEADS * SSD_HEAD_DIM
    rows = chunk * chunks_per_step
    kern = functools.partial(_ssd_kernel, chunk=chunk)
    const = lambda shape: pl.BlockSpec(shape, lambda bi, ci: (0,) * len(shape))
    return pl.pallas_call(
        kern,
        grid=(b, s // rows),
        in_specs=[
            pl.BlockSpec((1, rows, width), lambda bi, ci: (bi, ci, 0)),
            pl.BlockSpec((1, rows, LANES), lambda bi, ci: (bi, ci, 0)),
            const(dt_bias.shape), const(a_log.shape),
            const(d_full.shape), const(gnorm.shape), const(expand.shape),
        ],
        out_specs=pl.BlockSpec((1, rows, d_inner), lambda bi, ci: (bi, ci, 0)),
        out_shape=jax.ShapeDtypeStruct((b, s, d_inner), BF16),
        scratch_shapes=[pltpu.VMEM((SSD_GROUPS, SSD_STATE, d_inner // SSD_GROUPS), F32)],
        compiler_params=pltpu.CompilerParams(
            dimension_semantics=("arbitrary", "arbitrary"), vmem_limit_bytes=VMEM_LIMIT),
        name="ssd_scan",
    )(zx, dt_raw, dt_bias, a_log, d_full, gnorm, expand)


def _rope_tables(seq):
    half = HEAD_DIM // 2
    inv_freq = ROPE_THETA ** (-jnp.arange(half, dtype=F32) / half)
    ang = jnp.arange(seq, dtype=F32)[:, None] * inv_freq[None, :]
    cos, sin = jnp.cos(ang), jnp.sin(ang)
    reps = LANES // HEAD_DIM
    return jnp.tile(jnp.concatenate([cos, cos], axis=1), (1, reps)), jnp.tile(jnp.concatenate([-sin, sin], axis=1), (1, reps))


def _pad_lanes(v):
    return jnp.pad(v, (0, LANES - v.shape[0]))[None, :]


def _attn_layer(h2d, b, s, norm_w, w_in, lq1, lk1, lq2, lk2, subln, w_out, ffn, lambda_init,
                tm, tq_diff, tq_sb, sb_tiles):
    cos, sin = _rope_tables(s)
    proj = _attn_proj(h2d, norm_w[None, :], w_in.astype(BF16), cos, sin, s, tm)
    proj = proj.reshape(b, s, -1)
    lqk = jnp.stack([lq1, lk1, lq2, lk2])
    o_diff = _diff_attn(proj, lqk, subln[None, :], lambda_init, tq_diff)
    o_sb = _sb_attn(proj, tq_sb, sb_tiles)
    assert o_diff.shape[-1] == o_sb.shape[-1]
    w_out_bf = w_out.astype(BF16)
    return _mix_ffn(h2d, [o_diff.reshape(b * s, -1), o_sb.reshape(b * s, -1)], [w_out_bf, w_out_bf], *ffn, tm,
                    *_ffn_tiles(ffn[1].shape[2]))


def _ssd_layer(h2d, b, s, norm_w, w_in, conv_w, conv_b, dt_bias, a_log, d_skip, gnorm, w_out, ffn, tm, chunk):
    d_inner = SSD_HEADS * SSD_HEAD_DIM
    main = w_in.shape[1] - SSD_HEADS
    w_bf = w_in.astype(BF16)
    w_dt = jnp.pad(w_bf[:, main:], ((0, 0), (0, LANES - SSD_HEADS)))
    zx, dt_raw = _ssd_proj(h2d, norm_w[None, :], w_bf, w_dt, conv_w, conv_b[None, :], s, tm)
    expand = (jnp.arange(LANES)[:, None] == (jnp.arange(d_inner)[None, :] // SSD_HEAD_DIM)).astype(BF16)
    y = _ssd_scan(zx.reshape(b, s, main), dt_raw.reshape(b, s, LANES),
                  _pad_lanes(dt_bias), _pad_lanes(a_log), jnp.repeat(d_skip, SSD_HEAD_DIM)[None, :],
                  gnorm[None, :], expand, chunk, 4 if s % (4 * chunk) == 0 else 1)
    return _mix_ffn(h2d, [y.reshape(b * s, d_inner)], [w_out.astype(BF16)], *ffn, tm, *_ffn_tiles(ffn[1].shape[2]))


def kernel(x, attn_norm, attn_w_in, diff_lq1, diff_lk1, diff_lq2, diff_lk2, diff_subln, attn_w_out, ssd_norm, ssd_w_in, ssd_conv_w, ssd_conv_b, ssd_dt_bias, ssd_a_log, ssd_d, ssd_gnorm, ssd_w_out, ffn_norm, ffn_w_gate, ffn_w_up, ffn_w_down, final_norm):
    b, s, d = x.shape
    depth = ffn_norm.shape[0]
    tm = min(512, s)
    h = x.reshape(b * s, d)
    wg, wu, wd = ffn_w_gate.astype(BF16), ffn_w_up.astype(BF16), ffn_w_down.astype(BF16)
    for layer in range(depth):
        i = layer // 2
        fin = final_norm[None, :] if layer == depth - 1 else None
        ffn = (ffn_norm[layer][None, :], wg, wu, wd, layer, fin)
        if layer % 2 == 0:
            lambda_init = 0.8 - 0.6 * math.exp(-0.3 * layer)
            h = _attn_layer(h, b, s, attn_norm[i], attn_w_in[i], diff_lq1[i], diff_lk1[i], diff_lq2[i],
                            diff_lk2[i], diff_subln[i], attn_w_out[i], ffn, lambda_init,
                            tm=tm, tq_diff=min(512, s), tq_sb=min(256, s), sb_tiles=4 if s % 1024 == 0 else 1)
        else:
            h = _ssd_layer(h, b, s, ssd_norm[i], ssd_w_in[i], ssd_conv_w[i], ssd_conv_b[i], ssd_dt_bias[i],
                           ssd_a_log[i], ssd_d[i], ssd_gnorm[i], ssd_w_out[i], ffn,
                           tm=tm, chunk=128)
    return h.reshape(b, s, d)
```

```python
import functools
import math

import jax
import jax.numpy as jnp
from jax import lax
from jax.experimental import pallas as pl
from jax.experimental.pallas import tpu as pltpu

F32 = jnp.float32
BF16 = jnp.bfloat16

HEAD_DIM = 64
DIFF_HEADS = 4
SB_HEADS = 8
ROPE_THETA = 10000.0
NORM_EPS = 1e-6
SUBLN_EPS = 1e-5
SSD_HEADS = 32
SSD_GROUPS = 8
SSD_STATE = 128
SSD_CONV = 4
SSD_HEAD_DIM = 64
LANES = 128
SUBLANES = 8
VMEM_LIMIT = 56 * 1024 * 1024

LOG2E = 1.4426950408889634
QUERY_SCALE = LOG2E * HEAD_DIM ** -0.5
F32_EXP2_ZERO = -160.0


def _rms(x, g, eps):
    return x * lax.rsqrt(jnp.mean(x * x, axis=-1, keepdims=True) + eps) * g


def _silu(x):
    h = 0.5 * x
    return h + h * jnp.tanh(h)


def _softplus(x):
    return jnp.maximum(x, 0.0) + jnp.log1p(jnp.exp(-jnp.abs(x)))


def _split2(x):
    hi = x.astype(BF16)
    lo = (x - hi.astype(F32)).astype(BF16)
    return hi, lo


def _split3(x):
    hi = x.astype(BF16)
    r = x - hi.astype(F32)
    mid = r.astype(BF16)
    lo = (r - mid.astype(F32)).astype(BF16)
    return hi, mid, lo


def _dot(a, b):
    return jnp.dot(a, b, preferred_element_type=F32)


def _dot_nt(a, b):
    return lax.dot_general(a, b, (((1,), (1,)), ((), ())), preferred_element_type=F32)


MXU_COLS = 256


def _attn_proj_kernel(x_ref, g_ref, w_ref, cos_ref, sin_ref, o_ref, *, rope_cols, query_cols):
    xn = _rms(x_ref[...], g_ref[...], NORM_EPS).astype(BF16)
    tm = xn.shape[0]
    cos = cos_ref[...]
    sin = sin_ref[...]
    lane = lax.broadcasted_iota(jnp.int32, (tm, LANES), 1)
    first_half = (lane & (HEAD_DIM // 2)) == 0
    for c in range(w_ref.shape[1] // MXU_COLS):
        lo = c * MXU_COLS
        acc = _dot(xn, w_ref[:, lo:lo + MXU_COLS])
        is_query = any(a <= lo < b for a, b in query_cols)
        if lo < rope_cols:
            for s in range(MXU_COLS // LANES):
                t = acc[:, s * LANES:(s + 1) * LANES]
                partner = jnp.where(first_half,
                                    pltpu.roll(t, LANES - HEAD_DIM // 2, 1),
                                    pltpu.roll(t, HEAD_DIM // 2, 1))
                r = t * cos + partner * sin
                if is_query:
                    r = r * QUERY_SCALE
                o_ref[:, lo + s * LANES:lo + (s + 1) * LANES] = r.astype(o_ref.dtype)
        else:
            if is_query:
                acc = acc * QUERY_SCALE
            o_ref[:, lo:lo + MXU_COLS] = acc.astype(o_ref.dtype)


def _attn_proj(x2d, g, w_bf, cos, sin, seq, tm):
    m, d = x2d.shape
    n = w_bf.shape[1]
    diff_w = DIFF_HEADS * 2 * HEAD_DIM
    sb_w = SB_HEADS * HEAD_DIM
    rope_cols = 2 * diff_w
    query_cols = ((0, diff_w), (3 * diff_w, 3 * diff_w + sb_w))
    pos_blocks = seq // tm
    return pl.pallas_call(
        functools.partial(_attn_proj_kernel, rope_cols=rope_cols, query_cols=query_cols),
        grid=(m // tm,),
        in_specs=[
            pl.BlockSpec((tm, d), lambda i: (i, 0)),
            pl.BlockSpec((1, d), lambda i: (0, 0)),
            pl.BlockSpec((d, n), lambda i: (0, 0)),
            pl.BlockSpec((tm, LANES), lambda i: (i % pos_blocks, 0)),
            pl.BlockSpec((tm, LANES), lambda i: (i % pos_blocks, 0)),
        ],
        out_specs=pl.BlockSpec((tm, n), lambda i: (i, 0)),
        out_shape=jax.ShapeDtypeStruct((m, n), BF16),
        compiler_params=pltpu.CompilerParams(
            dimension_semantics=("arbitrary",), vmem_limit_bytes=VMEM_LIMIT),
        name="attn_proj",
    )(x2d, g, w_bf, cos, sin)


def _ssd_proj_kernel(x_ref, g_ref, w_ref, wdt_ref, cw_ref, cb_ref, o_ref, dt_ref, tail_ref, *, z_cols, pos_blocks):
    xn = _rms(x_ref[...], g_ref[...], NORM_EPS).astype(BF16)
    dt_ref[...] = _dot(xn, wdt_ref[...])
    tm = xn.shape[0]
    row8 = lax.broadcasted_iota(jnp.int32, (SUBLANES, MXU_COLS), 0)

    @pl.when(pl.program_id(0) % pos_blocks == 0)
    def _():
        tail_ref[...] = jnp.zeros(tail_ref.shape, F32)

    n_chunks = o_ref.shape[1] // MXU_COLS
    for c in range(n_chunks):
        lo = c * MXU_COLS
        acc = _dot(xn, w_ref[:, lo:lo + MXU_COLS])
        if lo < z_cols:
            out = _silu(acc)
        else:
            cl = lo - z_cols
            tail = tail_ref[:, cl:cl + MXU_COLS]
            tail_ref[:, cl:cl + MXU_COLS] = acc[tm - SUBLANES:, :]
            w0, w1, w2, w3 = (cw_ref[k:k + 1, cl:cl + MXU_COLS] for k in range(SSD_CONV))

            def shift(a, before, d):
                rolled = pltpu.roll(a, d, 0)
                top = jnp.where(row8 < d, pltpu.roll(before, d, 0), rolled[:SUBLANES])
                return jnp.concatenate([top, rolled[SUBLANES:]], axis=0)

            x1 = shift(acc, tail, 1)
            tail1 = pltpu.roll(tail, 1, 0)
            far = w1 * acc + w0 * x1
            far_tail = w1 * tail + w0 * tail1
            conv = cb_ref[:, cl:cl + MXU_COLS] + w3 * acc + w2 * x1 + shift(far, far_tail, 2)
            out = _silu(conv)
        o_ref[:, lo:lo + MXU_COLS] = out.astype(o_ref.dtype)


def _ssd_proj(x2d, g, w_bf, wdt_bf, conv_w, conv_b, seq, tm):
    m, d = x2d.shape
    conv_dim = conv_w.shape[1]
    n = SSD_HEADS * SSD_HEAD_DIM + conv_dim
    assert n % MXU_COLS == 0 and n <= w_bf.shape[1]
    kern = functools.partial(_ssd_proj_kernel, z_cols=n - conv_dim, pos_blocks=seq // tm)
    return pl.pallas_call(
        kern,
        grid=(m // tm,),
        in_specs=[
            pl.BlockSpec((tm, d), lambda i: (i, 0)),
            pl.BlockSpec((1, d), lambda i: (0, 0)),
            pl.BlockSpec(w_bf.shape, lambda i: (0, 0)),
            pl.BlockSpec((d, LANES), lambda i: (0, 0)),
            pl.BlockSpec(conv_w.shape, lambda i: (0, 0)),
            pl.BlockSpec(conv_b.shape, lambda i: (0, 0)),
        ],
        out_specs=[
            pl.BlockSpec((tm, n), lambda i: (i, 0)),
            pl.BlockSpec((tm, LANES), lambda i: (i, 0)),
        ],
        out_shape=[jax.ShapeDtypeStruct((m, n), BF16), jax.ShapeDtypeStruct((m, LANES), F32)],
        scratch_shapes=[pltpu.VMEM((SUBLANES, conv_dim), F32)],
        compiler_params=pltpu.CompilerParams(
            dimension_semantics=("arbitrary",), vmem_limit_bytes=VMEM_LIMIT),
        name="ssd_proj",
    )(x2d, g, w_bf, wdt_bf, conv_w, conv_b)


def _diff_attn_kernel(lqk_ref, q_ref, k_ref, v_ref, subln_ref, o_ref, vt_ref, s0_ref, s1_ref, m_ref, acc_ref, *,
                      tq, lambda_init):
    qi = pl.program_id(2)
    seq = k_ref.shape[1]
    gq = MXU_COLS
    halves = tq // gq

    @pl.when(qi == 0)
    def _():
        for c in range(seq // tq):
            vt_ref[:LANES, c * tq:(c + 1) * tq] = v_ref[0, c * tq:(c + 1) * tq, :].T
        vt_ref[LANES:, :] = jnp.ones((vt_ref.shape[0] - LANES, seq), BF16)

    q = q_ref[0]
    lane = lax.broadcasted_iota(jnp.int32, (tq, LANES), 1)
    zero = jnp.zeros_like(q)
    qq = jnp.concatenate([jnp.where(lane < HEAD_DIM, q, zero), jnp.where(lane >= HEAD_DIM, q, zero)], axis=0)
    nq = 2 * tq
    key_i = lax.broadcasted_iota(jnp.int32, (gq, nq), 0)
    qry_i = lax.broadcasted_iota(jnp.int32, (gq, nq), 1) & (tq - 1)

    s_refs = (s0_ref, s1_ref)

    def scores(kb, slot):
        start = pl.multiple_of(kb * gq, gq)
        s_refs[slot][...] = _dot_nt(k_ref[0, pl.ds(start, gq), :], qq)

    def consume(kb, slot, diag=None):
        start = pl.multiple_of(kb * gq, gq)
        s = s_refs[slot][...]
        if diag is not None:
            s = jnp.where(key_i + diag * gq <= qry_i, s, -jnp.inf)
        m = m_ref[...]
        m_new = jnp.maximum(m, jnp.max(s, axis=0, keepdims=True))
        p = jnp.exp2(s - m_new).astype(BF16)
        acc_ref[...] = jnp.exp2(m - m_new) * acc_ref[...] + _dot(vt_ref[:, pl.ds(start, gq)], p)
        m_ref[...] = m_new

    def pair(kb):
        scores(kb + 1, 1)
        consume(kb, 0)
        scores(kb + 2, 0)
        consume(kb + 1, 1)

    def pairs(first_pair, count):
        for c in range(count):
            pair(2 * (first_pair + c))

    unroll = 4

    def body(j, carry):
        pairs(unroll * j, unroll)
        return carry

    n_full = qi * halves
    m_ref[...] = jnp.full(m_ref.shape, -jnp.inf, F32)
    acc_ref[...] = jnp.zeros(acc_ref.shape, F32)
    scores(0, 0)
    lax.fori_loop(0, qi // unroll, body, 0)
    done = (qi // unroll) * unroll
    for bit in (2, 1):
        @pl.when((qi & bit) != 0)
        def _(bit=bit, done=done):
            pairs(done, bit)
        done = done + (qi & bit)

    scores(n_full + 1, 1)
    consume(n_full, 0, diag=0)
    consume(n_full + 1, 1, diag=1)

    lqk = lqk_ref[...]
    lam = (jnp.exp(jnp.sum(lqk[0:1] * lqk[1:2], axis=-1, keepdims=True))
           - jnp.exp(jnp.sum(lqk[2:3] * lqk[3:4], axis=-1, keepdims=True)) + lambda_init)
    acc = acc_ref[...]
    o_t = acc[:LANES] / acc[LANES:LANES + 1]
    o_t = o_t[:, :tq] - lam * o_t[:, tq:]
    o = _rms(o_t.T, subln_ref[...], SUBLN_EPS) * (1.0 - lambda_init)
    o_ref[0] = o.astype(o_ref.dtype)


def _diff_attn(proj, lqk, subln, lambda_init, tq):
    b, s, _ = proj.shape
    h = DIFF_HEADS
    assert tq == 2 * MXU_COLS and s % tq == 0, (tq, s)
    kern = functools.partial(_diff_attn_kernel, tq=tq, lambda_init=lambda_init)
    return pl.pallas_call(
        kern,
        grid=(b, h, s // tq),
        in_specs=[
            pl.BlockSpec((4, HEAD_DIM), lambda bi, hi, qi: (0, 0)),
            pl.BlockSpec((1, tq, LANES), lambda bi, hi, qi: (bi, qi, hi)),
            pl.BlockSpec((1, s, LANES), lambda bi, hi, qi: (bi, 0, h + hi)),
            pl.BlockSpec((1, s, LANES), lambda bi, hi, qi: (bi, 0, 2 * h + hi)),
            pl.BlockSpec((1, LANES), lambda bi, hi, qi: (0, 0)),
        ],
        out_specs=pl.BlockSpec((1, tq, LANES), lambda bi, hi, qi: (bi, qi, hi)),
        out_shape=jax.ShapeDtypeStruct((b, s, h * LANES), BF16),
        scratch_shapes=[pltpu.VMEM((LANES + 2 * SUBLANES, s), BF16),
                        pltpu.VMEM((MXU_COLS, 2 * tq), F32), pltpu.VMEM((MXU_COLS, 2 * tq), F32),
                        pltpu.VMEM((1, 2 * tq), F32), pltpu.VMEM((LANES + 2 * SUBLANES, 2 * tq), F32)],
        compiler_params=pltpu.CompilerParams(
            dimension_semantics=("arbitrary", "arbitrary", "arbitrary"), vmem_limit_bytes=VMEM_LIMIT),
        name="diff_attn",
    )(lqk, proj, proj, proj, subln)


def _sb_attn_kernel(q_ref, k_ref, v_ref, o_ref, *, tq, tiles):
    first = pl.program_id(2) * tiles
    rows = 2 * tiles * tq
    lane = lax.broadcasted_iota(jnp.int32, (tq, LANES), 1)
    row = lax.broadcasted_iota(jnp.int32, (rows, tq), 0)
    col = lax.broadcasted_iota(jnp.int32, (rows, tq), 1)
    strict = col < (row & (tq - 1))
    suffix = jnp.where(lax.broadcasted_iota(jnp.int32, (tq, tq), 1) <= lax.broadcasted_iota(jnp.int32, (tq, tq), 0),
                       1.0, 0.0).astype(BF16)
    tile_of_row = lax.broadcasted_iota(jnp.int32, (rows, 1), 0) // (2 * tq)

    qqs = []
    for r in range(tiles):
        q = q_ref[0, r * tq:(r + 1) * tq, :]
        zero = jnp.zeros_like(q)
        qqs.append(jnp.concatenate([jnp.where(lane < HEAD_DIM, q, zero), jnp.where(lane >= HEAD_DIM, q, zero)], axis=0))

    def step(i, run, acc, masked):
        vs, zs = [], []
        for r in range(tiles):
            start = pl.multiple_of(jnp.maximum(first + r - i, 0) * tq, tq)
            zs.append(_dot_nt(qqs[r], k_ref[0, pl.ds(start, tq), :]))
            vs.append(v_ref[0, pl.ds(start, tq), :])
        z = jnp.concatenate(zs, axis=0)
        log_keep = -(jnp.maximum(z, 0.0) + jnp.log2(1.0 + jnp.exp2(-jnp.abs(z))))
        if masked:
            log_keep = jnp.where(strict, log_keep, 0.0)
        csum = _dot(log_keep.astype(BF16), suffix)
        run_in = run if masked else jnp.where(tile_of_row >= i - first, run, -1e30)
        a = jnp.exp2(z + csum + run_in)
        if masked:
            a = jnp.where(strict, a, 0.0)
        a = a.astype(BF16)
        pv = [_dot(a[r * 2 * tq:(r + 1) * 2 * tq], vs[r]) for r in range(tiles)]
        return run_in + csum[:, 0:1], acc + jnp.concatenate(pv, axis=0)

    run, acc = step(0, jnp.zeros((rows, 1), F32), jnp.zeros((rows, LANES), F32), True)
    run, acc = step(1, run, acc, False)

    def cond(st):
        i, run, _ = st
        return jnp.logical_and(first + tiles - 1 - i >= 0, jnp.max(run) > F32_EXP2_ZERO)

    def body(st):
        i, run, acc = st
        run, acc = step(i, run, acc, False)
        return i + 1, run, acc

    _, _, acc = lax.while_loop(cond, body, (jnp.int32(2), run, acc))
    for r in range(tiles):
        o_ref[0, r * tq:(r + 1) * tq, :] = jnp.where(
            lane < HEAD_DIM, acc[2 * r * tq:(2 * r + 1) * tq], acc[(2 * r + 1) * tq:(2 * r + 2) * tq]).astype(o_ref.dtype)


def _sb_attn(proj, tq, tiles):
    b, s, _ = proj.shape
    pairs = SB_HEADS * HEAD_DIM // LANES
    base = 3 * DIFF_HEADS * 2 * HEAD_DIM // LANES
    kern = functools.partial(_sb_attn_kernel, tq=tq, tiles=tiles)
    rows = tq * tiles
    return pl.pallas_call(
        kern,
        grid=(b, pairs, s // rows),
        in_specs=[
            pl.BlockSpec((1, rows, LANES), lambda bi, hi, qi: (bi, qi, base + hi)),
            pl.BlockSpec((1, s, LANES), lambda bi, hi, qi: (bi, 0, base + pairs + hi)),
            pl.BlockSpec((1, s, LANES), lambda bi, hi, qi: (bi, 0, base + 2 * pairs + hi)),
        ],
        out_specs=pl.BlockSpec((1, rows, LANES), lambda bi, hi, qi: (bi, qi, hi)),
        out_shape=jax.ShapeDtypeStruct((b, s, pairs * LANES), BF16),
        compiler_params=pltpu.CompilerParams(
            dimension_semantics=("arbitrary", "arbitrary", "arbitrary"), vmem_limit_bytes=VMEM_LIMIT),
        name="sb_attn",
    )(proj, proj, proj)


def _mix_ffn_kernel(*refs, n_y, sub, final, ffn_width, n_steps):
    h_ref = refs[0]
    y_refs = refs[1:1 + n_y]
    wo_refs = refs[1 + n_y:1 + 2 * n_y]
    fg_ref, wg_ref, wu_ref, wd_ref = refs[1 + 2 * n_y:5 + 2 * n_y]
    fin_ref = refs[5 + 2 * n_y] if final else None
    o_ref, n_ref, acc_ref = refs[-3:]
    j = pl.program_id(1)

    @pl.when(j == 0)
    def _():
        h1 = h_ref[...]
        for y_ref, wo_ref in zip(y_refs, wo_refs):
            h1 = h1 + _dot(y_ref[...], wo_ref[...])
        n_ref[...] = _rms(h1, fg_ref[...], NORM_EPS).astype(BF16)
        acc_ref[...] = h1

    n = n_ref[...]
    fc = wg_ref.shape[1]
    last_j = pl.num_programs(1) - 1

    def chunk(lo):
        g = _dot(n, wg_ref[:, lo:lo + sub])
        u = _dot(n, wu_ref[:, lo:lo + sub])
        return _dot((_silu(g) * u).astype(BF16), wd_ref[lo:lo + sub, :])

    chunks = list(range(0, fc, sub))
    beyond = [lo for lo in chunks if (n_steps - 1) * fc + lo + sub > ffn_width]
    for lo in beyond:
        @pl.when(j < last_j)
        def _(lo=lo):
            acc_ref[...] += chunk(lo)

    acc = acc_ref[...]
    for lo in chunks:
        if lo not in beyond:
            acc = acc + chunk(lo)
    acc_ref[...] = acc

    @pl.when(j == last_j)
    def _():
        out = acc_ref[...]
        if final:
            out = _rms(out, fin_ref[...], NORM_EPS)
        o_ref[...] = out


def _ffn_tiles(f):
    sub = MXU_COLS
    assert f % sub == 0, f
    n_sub = f // sub
    return -(-n_sub // 2) * sub, sub


def _mix_ffn(h2d, ys, wos, fg, wg, wu, wd, layer, fin, tm, fc, sub):
    m, d = h2d.shape
    f = wg.shape[2]
    n_steps = -(-f // fc)
    final = fin is not None
    kern = functools.partial(_mix_ffn_kernel, n_y=len(ys), sub=sub, final=final, ffn_width=f, n_steps=n_steps)
    const = lambda shape: pl.BlockSpec(shape, lambda i, j: (0,) * len(shape))
    in_specs = [pl.BlockSpec((tm, d), lambda i, j: (i, 0))]
    in_specs += [pl.BlockSpec((tm, y.shape[1]), lambda i, j: (i, 0)) for y in ys]
    in_specs += [pl.BlockSpec((y.shape[1], d), lambda i, j, k=k: (k, 0)) for k, y in enumerate(ys)]
    in_specs += [const(fg.shape),
                 pl.BlockSpec((None, d, fc), lambda i, j: (layer, 0, j)),
                 pl.BlockSpec((None, d, fc), lambda i, j: (layer, 0, j)),
                 pl.BlockSpec((None, fc, d), lambda i, j: (layer, j, 0))]
    args = [h2d, *ys, *wos, fg, wg, wu, wd]
    if final:
        in_specs.append(const(fin.shape))
        args.append(fin)
    return pl.pallas_call(
        kern,
        grid=(m // tm, n_steps),
        in_specs=in_specs,
        out_specs=pl.BlockSpec((tm, d), lambda i, j: (i, 0)),
        out_shape=jax.ShapeDtypeStruct((m, d), F32),
        scratch_shapes=[pltpu.VMEM((tm, d), BF16), pltpu.VMEM((tm, d), F32)],
        compiler_params=pltpu.CompilerParams(
            dimension_semantics=("arbitrary", "arbitrary"), vmem_limit_bytes=VMEM_LIMIT),
        name="mix_ffn",
    )(*args)


def _ssd_kernel(zx_ref, dt_ref, dtb_ref, alog_ref, dsk_ref, gn_ref, exp_ref, o_ref, state, *, chunk):
    L = chunk
    gw = SSD_HEADS * SSD_HEAD_DIM // SSD_GROUPS

    @pl.when(pl.program_id(1) == 0)
    def _():
        state[...] = jnp.zeros(state.shape, F32)

    lane = lax.broadcasted_iota(jnp.int32, (1, LANES), 1)
    a = jnp.where(lane < SSD_HEADS, -jnp.exp(alog_ref[...]), 0.0)
    row = lax.broadcasted_iota(jnp.int32, (L, L), 0)
    col = lax.broadcasted_iota(jnp.int32, (L, L), 1)
    lower = row >= col
    tril = jnp.where(lower, 1.0, 0.0).astype(BF16)
    e = exp_ref[...]
    glane = lax.broadcasted_iota(jnp.int32, (1, gw), 1)
    for r0 in range(0, zx_ref.shape[1], L):
        _ssd_chunk(zx_ref, dt_ref, dtb_ref, dsk_ref, gn_ref, o_ref, state, r0, L, a, lower, tril, e, glane)


def _ssd_chunk(zx_ref, dt_ref, dtb_ref, dsk_ref, gn_ref, o_ref, state, r0, L, a, lower, tril, e, glane):
    d_inner = SSD_HEADS * SSD_HEAD_DIM
    gw = d_inner // SSD_GROUPS
    hpg = SSD_HEADS // SSD_GROUPS
    x_off, b_off, c_off = d_inner, 2 * d_inner, 2 * d_inner + SSD_GROUPS * SSD_STATE
    rows = slice(r0, r0 + L)
    dt = _softplus(dt_ref[0, rows, :] + dtb_ref[...])
    d_hi, d_mid, d_lo = _split3(dt * a)
    acum = _dot(tril, d_hi) + _dot(tril, d_mid) + _dot(tril, d_lo)
    last = acum[L - 1:L, :]
    ea = jnp.exp(acum)
    w = jnp.exp(last - acum) * dt
    acum_t = acum.T
    dt_t = dt.T
    w_t = w.T
    c_hi, c_mid, c_lo = _split3(jnp.broadcast_to(jnp.exp(last), (SUBLANES, LANES)))
    cd_full = (_dot(c_hi, e) + _dot(c_mid, e) + _dot(c_lo, e))[0:1, :]

    for g in range(SSD_GROUPS):
        xg_bf = zx_ref[0, rows, x_off + g * gw:x_off + (g + 1) * gw]
        xg = xg_bf.astype(F32)
        bg_bf = zx_ref[0, rows, b_off + g * SSD_STATE:b_off + (g + 1) * SSD_STATE]
        cg_bf = zx_ref[0, rows, c_off + g * SSD_STATE:c_off + (g + 1) * SSD_STATE]
        cg = cg_bf.astype(F32)
        cb = _dot_nt(cg_bf, bg_bf)
        bg_t = bg_bf.astype(F32).T
        prev = state[g]
        prev_bf = prev.astype(BF16)
        y_lhs, y_rhs, st_lhs, st_rhs = [], [], [], []
        for j in range(hpg):
            h = g * hpg + j
            in_head = jnp.logical_and(glane >= j * SSD_HEAD_DIM, glane < (j + 1) * SSD_HEAD_DIM)
            x_h = jnp.where(in_head, xg_bf, jnp.zeros_like(xg_bf))
            prev_h = jnp.where(in_head, prev_bf, jnp.zeros_like(prev_bf))
            seg = acum[:, h:h + 1] - acum_t[h:h + 1, :]
            decay = jnp.exp(jnp.where(lower, seg, -jnp.inf))
            y_lhs += [(cb * decay * dt_t[h:h + 1, :]).astype(BF16), (cg * ea[:, h:h + 1]).astype(BF16)]
            y_rhs += [x_h, prev_h]
            st_lhs.append((bg_t * w_t[h:h + 1, :]).astype(BF16))
            st_rhs.append(x_h)
        y = dsk_ref[:, g * gw:(g + 1) * gw] * xg + _dot(jnp.concatenate(y_lhs, axis=1), jnp.concatenate(y_rhs, axis=0))
        state[g] = prev * cd_full[:, g * gw:(g + 1) * gw] + _dot(
            jnp.concatenate(st_lhs, axis=1), jnp.concatenate(st_rhs, axis=0))
        gated = y * zx_ref[0, rows, g * gw:(g + 1) * gw].astype(F32)
        o_ref[0, rows, g * gw:(g + 1) * gw] = _rms(
            gated, gn_ref[:, g * gw:(g + 1) * gw], SUBLN_EPS).astype(o_ref.dtype)


def _ssd_scan(zx, dt_raw, dt_bias, a_log, d_full, gnorm, expand, chunk, chunks_per_step):
    b, s, width = zx.shape
    d_inner = SSD_HEADS * SSD_HEAD_DIM
    rows = chunk * chunks_per_step
    kern = functools.partial(_ssd_kernel, chunk=chunk)
    const = lambda shape: pl.BlockSpec(shape, lambda bi, ci: (0,) * len(shape))
    return pl.pallas_call(
        kern,
        grid=(b, s // rows),
        in_specs=[
            pl.BlockSpec((1, rows, width), lambda bi, ci: (bi, ci, 0)),
            pl.BlockSpec((1, rows, LANES), lambda bi, ci: (bi, ci, 0)),
            const(dt_bias.shape), const(a_log.shape),
            const(d_full.shape), const(gnorm.shape), const(expand.shape),
        ],
        out_specs=pl.BlockSpec((1, rows, d_inner), lambda bi, ci: (bi, ci, 0)),
        out_shape=jax.ShapeDtypeStruct((b, s, d_inner), BF16),
        scratch_shapes=[pltpu.VMEM((SSD_GROUPS, SSD_STATE, d_inner // SSD_GROUPS), F32)],
        compiler_params=pltpu.CompilerParams(
            dimension_semantics=("arbitrary", "arbitrary"), vmem_limit_bytes=VMEM_LIMIT),
        name="ssd_scan",
    )(zx, dt_raw, dt_bias, a_log, d_full, gnorm, expand)


def _rope_tables(seq):
    half = HEAD_DIM // 2
    inv_freq = ROPE_THETA ** (-jnp.arange(half, dtype=F32) / half)
    ang = jnp.arange(seq, dtype=F32)[:, None] * inv_freq[None, :]
    cos, sin = jnp.cos(ang), jnp.sin(ang)
    reps = LANES // HEAD_DIM
    return jnp.tile(jnp.concatenate([cos, cos], axis=1), (1, reps)), jnp.tile(jnp.concatenate([-sin, sin], axis=1), (1, reps))


def _pad_lanes(v):
    return jnp.pad(v, (0, LANES - v.shape[0]))[None, :]


def _attn_layer(h2d, b, s, norm_w, w_in, lq1, lk1, lq2, lk2, subln, w_out, ffn, lambda_init,
                tm, tq_diff, tq_sb, sb_tiles):
    cos, sin = _rope_tables(s)
    proj = _attn_proj(h2d, norm_w[None, :], w_in.astype(BF16), cos, sin, s, tm)
    proj = proj.reshape(b, s, -1)
    lqk = jnp.stack([lq1, lk1, lq2, lk2])
    o_diff = _diff_attn(proj, lqk, subln[None, :], lambda_init, tq_diff)
    o_sb = _sb_attn(proj, tq_sb, sb_tiles)
    assert o_diff.shape[-1] == o_sb.shape[-1]
    w_out_bf = w_out.astype(BF16)
    return _mix_ffn(h2d, [o_diff.reshape(b * s, -1), o_sb.reshape(b * s, -1)], [w_out_bf, w_out_bf], *ffn, tm,
                    *_ffn_tiles(ffn[1].shape[2]))


def _ssd_layer(h2d, b, s, norm_w, w_in, conv_w, conv_b, dt_bias, a_log, d_skip, gnorm, w_out, ffn, tm, chunk):
    d_inner = SSD_HEADS * SSD_HEAD_DIM
    main = w_in.shape[1] - SSD_HEADS
    w_bf = w_in.astype(BF16)
    w_dt = jnp.pad(w_bf[:, main:], ((0, 0), (0, LANES - SSD_HEADS)))
    zx, dt_raw = _ssd_proj(h2d, norm_w[None, :], w_bf, w_dt, conv_w, conv_b[None, :], s, tm)
    expand = (jnp.arange(LANES)[:, None] == (jnp.arange(d_inner)[None, :] // SSD_HEAD_DIM)).astype(BF16)
    y = _ssd_scan(zx.reshape(b, s, main), dt_raw.reshape(b, s, LANES),
                  _pad_lanes(dt_bias), _pad_lanes(a_log), jnp.repeat(d_skip, SSD_HEAD_DIM)[None, :],
                  gnorm[None, :], expand, chunk, 4 if s % (4 * chunk) == 0 else 1)
    return _mix_ffn(h2d, [y.reshape(b * s, d_inner)], [w_out.astype(BF16)], *ffn, tm, *_ffn_tiles(ffn[1].shape[2]))


def kernel(x, attn_norm, attn_w_in, diff_lq1, diff_lk1, diff_lq2, diff_lk2, diff_subln, attn_w_out, ssd_norm, ssd_w_in, ssd_conv_w, ssd_conv_b, ssd_dt_bias, ssd_a_log, ssd_d, ssd_gnorm, ssd_w_out, ffn_norm, ffn_w_gate, ffn_w_up, ffn_w_down, final_norm):
    b, s, d = x.shape
    depth = ffn_norm.shape[0]
    tm = min(512, s)
    h = x.reshape(b * s, d)
    wg, wu, wd = ffn_w_gate.astype(BF16), ffn_w_up.astype(BF16), ffn_w_down.astype(BF16)
    for layer in range(depth):
        i = layer // 2
        fin = final_norm[None, :] if layer == depth - 1 else None
        ffn = (ffn_norm[layer][None, :], wg, wu, wd, layer, fin)
        if layer % 2 == 0:
            lambda_init = 0.8 - 0.6 * math.exp(-0.3 * layer)
            h = _attn_layer(h, b, s, attn_norm[i], attn_w_in[i], diff_lq1[i], diff_lk1[i], diff_lq2[i],
                            diff_lk2[i], diff_subln[i], attn_w_out[i], ffn, lambda_init,
                            tm=tm, tq_diff=min(512, s), tq_sb=min(256, s), sb_tiles=4 if s % 1024 == 0 else 1)
        else:
            h = _ssd_layer(h, b, s, ssd_norm[i], ssd_w_in[i], ssd_conv_w[i], ssd_conv_b[i], ssd_dt_bias[i],
                           ssd_a_log[i], ssd_d[i], ssd_gnorm[i], ssd_w_out[i], ffn,
                           tm=tm, chunk=128)
    return h.reshape(b, s, d)
```

```python
import functools
import math

import jax
import jax.numpy as jnp
from jax import lax
from jax.experimental import pallas as pl
from jax.experimental.pallas import tpu as pltpu

F32 = jnp.float32
BF16 = jnp.bfloat16

HEAD_DIM = 64
DIFF_HEADS = 4
SB_HEADS = 8
ROPE_THETA = 10000.0
NORM_EPS = 1e-6
SUBLN_EPS = 1e-5
SSD_HEADS = 32
SSD_GROUPS = 8
SSD_STATE = 128
SSD_CONV = 4
SSD_HEAD_DIM = 64
LANES = 128
SUBLANES = 8
VMEM_LIMIT = 58 * 1024 * 1024

LOG2E = 1.4426950408889634
QUERY_SCALE = LOG2E * HEAD_DIM ** -0.5
F32_EXP2_ZERO = -160.0


def _rms(x, g, eps):
    return x * lax.rsqrt(jnp.mean(x * x, axis=-1, keepdims=True) + eps) * g


def _silu(x):
    h = 0.5 * x
    return h + h * jnp.tanh(h)


def _softplus(x):
    return jnp.maximum(x, 0.0) + jnp.log1p(jnp.exp(-jnp.abs(x)))


def _split2(x):
    hi = x.astype(BF16)
    lo = (x - hi.astype(F32)).astype(BF16)
    return hi, lo


def _split3(x):
    hi = x.astype(BF16)
    r = x - hi.astype(F32)
    mid = r.astype(BF16)
    lo = (r - mid.astype(F32)).astype(BF16)
    return hi, mid, lo


def _dot(a, b):
    return jnp.dot(a, b, preferred_element_type=F32)


def _dot_nt(a, b):
    return lax.dot_general(a, b, (((1,), (1,)), ((), ())), preferred_element_type=F32)


MXU_COLS = 256


def _attn_proj_kernel(x_ref, g_ref, w_ref, cos_ref, sin_ref, o_ref, *, rope_cols, query_cols):
    xn = _rms(x_ref[...], g_ref[...], NORM_EPS).astype(BF16)
    tm = xn.shape[0]
    cos = cos_ref[...]
    sin = sin_ref[...]
    lane = lax.broadcasted_iota(jnp.int32, (tm, LANES), 1)
    first_half = (lane & (HEAD_DIM // 2)) == 0
    for c in range(w_ref.shape[1] // MXU_COLS):
        lo = c * MXU_COLS
        acc = _dot(xn, w_ref[:, lo:lo + MXU_COLS])
        is_query = any(a <= lo < b for a, b in query_cols)
        if lo < rope_cols:
            for s in range(MXU_COLS // LANES):
                t = acc[:, s * LANES:(s + 1) * LANES]
                partner = jnp.where(first_half,
                                    pltpu.roll(t, LANES - HEAD_DIM // 2, 1),
                                    pltpu.roll(t, HEAD_DIM // 2, 1))
                r = t * cos + partner * sin
                if is_query:
                    r = r * QUERY_SCALE
                o_ref[:, lo + s * LANES:lo + (s + 1) * LANES] = r.astype(o_ref.dtype)
        else:
            if is_query:
                acc = acc * QUERY_SCALE
            o_ref[:, lo:lo + MXU_COLS] = acc.astype(o_ref.dtype)


def _attn_proj(x2d, g, w_bf, cos, sin, seq, tm):
    m, d = x2d.shape
    n = w_bf.shape[1]
    diff_w = DIFF_HEADS * 2 * HEAD_DIM
    sb_w = SB_HEADS * HEAD_DIM
    rope_cols = 2 * diff_w
    query_cols = ((0, diff_w), (3 * diff_w, 3 * diff_w + sb_w))
    pos_blocks = seq // tm
    return pl.pallas_call(
        functools.partial(_attn_proj_kernel, rope_cols=rope_cols, query_cols=query_cols),
        grid=(m // tm,),
        in_specs=[
            pl.BlockSpec((tm, d), lambda i: (i, 0)),
            pl.BlockSpec((1, d), lambda i: (0, 0)),
            pl.BlockSpec((d, n), lambda i: (0, 0)),
            pl.BlockSpec((tm, LANES), lambda i: (i % pos_blocks, 0)),
            pl.BlockSpec((tm, LANES), lambda i: (i % pos_blocks, 0)),
        ],
        out_specs=pl.BlockSpec((tm, n), lambda i: (i, 0)),
        out_shape=jax.ShapeDtypeStruct((m, n), BF16),
        compiler_params=pltpu.CompilerParams(
            dimension_semantics=("arbitrary",), vmem_limit_bytes=VMEM_LIMIT),
        name="attn_proj",
    )(x2d, g, w_bf, cos, sin)


def _ssd_proj_kernel(x_ref, g_ref, w_ref, wdt_ref, cw_ref, cb_ref, o_ref, dt_ref, tail_ref, *, z_cols, pos_blocks):
    xn = _rms(x_ref[...], g_ref[...], NORM_EPS).astype(BF16)
    dt_ref[...] = _dot(xn, wdt_ref[...])
    tm = xn.shape[0]
    row8 = lax.broadcasted_iota(jnp.int32, (SUBLANES, MXU_COLS), 0)

    @pl.when(pl.program_id(0) % pos_blocks == 0)
    def _():
        tail_ref[...] = jnp.zeros(tail_ref.shape, F32)

    n_chunks = o_ref.shape[1] // MXU_COLS
    for c in range(n_chunks):
        lo = c * MXU_COLS
        acc = _dot(xn, w_ref[:, lo:lo + MXU_COLS])
        if lo < z_cols:
            out = _silu(acc)
        else:
            cl = lo - z_cols
            tail = tail_ref[:, cl:cl + MXU_COLS]
            tail_ref[:, cl:cl + MXU_COLS] = acc[tm - SUBLANES:, :]
            w0, w1, w2, w3 = (cw_ref[k:k + 1, cl:cl + MXU_COLS] for k in range(SSD_CONV))

            def shift(a, before, d):
                rolled = pltpu.roll(a, d, 0)
                top = jnp.where(row8 < d, pltpu.roll(before, d, 0), rolled[:SUBLANES])
                return jnp.concatenate([top, rolled[SUBLANES:]], axis=0)

            x1 = shift(acc, tail, 1)
            tail1 = pltpu.roll(tail, 1, 0)
            far = w1 * acc + w0 * x1
            far_tail = w1 * tail + w0 * tail1
            conv = cb_ref[:, cl:cl + MXU_COLS] + w3 * acc + w2 * x1 + shift(far, far_tail, 2)
            out = _silu(conv)
        o_ref[:, lo:lo + MXU_COLS] = out.astype(o_ref.dtype)


def _ssd_proj(x2d, g, w_bf, wdt_bf, conv_w, conv_b, seq, tm):
    m, d = x2d.shape
    conv_dim = conv_w.shape[1]
    n = SSD_HEADS * SSD_HEAD_DIM + conv_dim
    assert n % MXU_COLS == 0 and n <= w_bf.shape[1]
    kern = functools.partial(_ssd_proj_kernel, z_cols=n - conv_dim, pos_blocks=seq // tm)
    return pl.pallas_call(
        kern,
        grid=(m // tm,),
        in_specs=[
            pl.BlockSpec((tm, d), lambda i: (i, 0)),
            pl.BlockSpec((1, d), lambda i: (0, 0)),
            pl.BlockSpec(w_bf.shape, lambda i: (0, 0)),
            pl.BlockSpec((d, LANES), lambda i: (0, 0)),
            pl.BlockSpec(conv_w.shape, lambda i: (0, 0)),
            pl.BlockSpec(conv_b.shape, lambda i: (0, 0)),
        ],
        out_specs=[
            pl.BlockSpec((tm, n), lambda i: (i, 0)),
            pl.BlockSpec((tm, LANES), lambda i: (i, 0)),
        ],
        out_shape=[jax.ShapeDtypeStruct((m, n), BF16), jax.ShapeDtypeStruct((m, LANES), F32)],
        scratch_shapes=[pltpu.VMEM((SUBLANES, conv_dim), F32)],
        compiler_params=pltpu.CompilerParams(
            dimension_semantics=("arbitrary",), vmem_limit_bytes=VMEM_LIMIT),
        name="ssd_proj",
    )(x2d, g, w_bf, wdt_bf, conv_w, conv_b)


def _diff_attn_kernel(lqk_ref, q_ref, k_ref, v_ref, subln_ref, o_ref, vt_ref, s0_ref, s1_ref, m_ref, acc_ref, *,
                      tq, lambda_init):
    qi = pl.program_id(2)
    seq = k_ref.shape[1]
    gq = MXU_COLS
    halves = tq // gq

    @pl.when(qi == 0)
    def _():
        for c in range(seq // tq):
            vt_ref[:LANES, c * tq:(c + 1) * tq] = v_ref[0, c * tq:(c + 1) * tq, :].T
        vt_ref[LANES:, :] = jnp.ones((vt_ref.shape[0] - LANES, seq), BF16)

    q = q_ref[0]
    lane = lax.broadcasted_iota(jnp.int32, (tq, LANES), 1)
    zero = jnp.zeros_like(q)
    qq = jnp.concatenate([jnp.where(lane < HEAD_DIM, q, zero), jnp.where(lane >= HEAD_DIM, q, zero)], axis=0)
    nq = 2 * tq
    key_i = lax.broadcasted_iota(jnp.int32, (gq, nq), 0)
    qry_i = lax.broadcasted_iota(jnp.int32, (gq, nq), 1) & (tq - 1)

    s_refs = (s0_ref, s1_ref)

    def scores(kb, slot):
        start = pl.multiple_of(kb * gq, gq)
        s_refs[slot][...] = _dot_nt(k_ref[0, pl.ds(start, gq), :], qq)

    def consume(kb, slot, diag=None):
        start = pl.multiple_of(kb * gq, gq)
        s = s_refs[slot][...]
        if diag is not None:
            s = jnp.where(key_i + diag * gq <= qry_i, s, -jnp.inf)
        m = m_ref[...]
        m_new = jnp.maximum(m, jnp.max(s, axis=0, keepdims=True))
        p = jnp.exp2(s - m_new).astype(BF16)
        acc_ref[...] = jnp.exp2(m - m_new) * acc_ref[...] + _dot(vt_ref[:, pl.ds(start, gq)], p)
        m_ref[...] = m_new

    def pair(kb):
        scores(kb + 1, 1)
        consume(kb, 0)
        scores(kb + 2, 0)
        consume(kb + 1, 1)

    def pairs(first_pair, count):
        for c in range(count):
            pair(2 * (first_pair + c))

    unroll = 4

    def body(j, carry):
        pairs(unroll * j, unroll)
        return carry

    n_full = qi * halves
    m_ref[...] = jnp.full(m_ref.shape, -jnp.inf, F32)
    acc_ref[...] = jnp.zeros(acc_ref.shape, F32)
    scores(0, 0)
    lax.fori_loop(0, qi // unroll, body, 0)
    done = (qi // unroll) * unroll
    for bit in (2, 1):
        @pl.when((qi & bit) != 0)
        def _(bit=bit, done=done):
            pairs(done, bit)
        done = done + (qi & bit)

    scores(n_full + 1, 1)
    consume(n_full, 0, diag=0)
    consume(n_full + 1, 1, diag=1)

    lqk = lqk_ref[...]
    lam = (jnp.exp(jnp.sum(lqk[0:1] * lqk[1:2], axis=-1, keepdims=True))
           - jnp.exp(jnp.sum(lqk[2:3] * lqk[3:4], axis=-1, keepdims=True)) + lambda_init)
    acc = acc_ref[...]
    o_t = acc[:LANES] / acc[LANES:LANES + 1]
    o_t = o_t[:, :tq] - lam * o_t[:, tq:]
    o = _rms(o_t.T, subln_ref[...], SUBLN_EPS) * (1.0 - lambda_init)
    o_ref[0] = o.astype(o_ref.dtype)


def _diff_attn(proj, lqk, subln, lambda_init, tq):
    b, s, _ = proj.shape
    h = DIFF_HEADS
    assert tq == 2 * MXU_COLS and s % tq == 0, (tq, s)
    kern = functools.partial(_diff_attn_kernel, tq=tq, lambda_init=lambda_init)
    return pl.pallas_call(
        kern,
        grid=(b, h, s // tq),
        in_specs=[
            pl.BlockSpec((4, HEAD_DIM), lambda bi, hi, qi: (0, 0)),
            pl.BlockSpec((1, tq, LANES), lambda bi, hi, qi: (bi, qi, hi)),
            pl.BlockSpec((1, s, LANES), lambda bi, hi, qi: (bi, 0, h + hi)),
            pl.BlockSpec((1, s, LANES), lambda bi, hi, qi: (bi, 0, 2 * h + hi)),
            pl.BlockSpec((1, LANES), lambda bi, hi, qi: (0, 0)),
        ],
        out_specs=pl.BlockSpec((1, tq, LANES), lambda bi, hi, qi: (bi, qi, hi)),
        out_shape=jax.ShapeDtypeStruct((b, s, h * LANES), BF16),
        scratch_shapes=[pltpu.VMEM((LANES + 2 * SUBLANES, s), BF16),
                        pltpu.VMEM((MXU_COLS, 2 * tq), F32), pltpu.VMEM((MXU_COLS, 2 * tq), F32),
                        pltpu.VMEM((1, 2 * tq), F32), pltpu.VMEM((LANES + 2 * SUBLANES, 2 * tq), F32)],
        compiler_params=pltpu.CompilerParams(
            dimension_semantics=("arbitrary", "arbitrary", "arbitrary"), vmem_limit_bytes=VMEM_LIMIT),
        name="diff_attn",
    )(lqk, proj, proj, proj, subln)


def _sb_attn_kernel(q_ref, k_ref, v_ref, o_ref, *, tq, tiles):
    first = pl.program_id(2) * tiles
    rows = 2 * tiles * tq
    lane = lax.broadcasted_iota(jnp.int32, (tq, LANES), 1)
    row = lax.broadcasted_iota(jnp.int32, (rows, tq), 0)
    col = lax.broadcasted_iota(jnp.int32, (rows, tq), 1)
    strict = col < (row & (tq - 1))
    suffix = jnp.where(lax.broadcasted_iota(jnp.int32, (tq, tq), 1) <= lax.broadcasted_iota(jnp.int32, (tq, tq), 0),
                       1.0, 0.0).astype(BF16)
    tile_of_row = lax.broadcasted_iota(jnp.int32, (rows, 1), 0) // (2 * tq)

    qqs = []
    for r in range(tiles):
        q = q_ref[0, r * tq:(r + 1) * tq, :]
        zero = jnp.zeros_like(q)
        qqs.append(jnp.concatenate([jnp.where(lane < HEAD_DIM, q, zero), jnp.where(lane >= HEAD_DIM, q, zero)], axis=0))

    def step(i, run, acc, masked):
        vs, zs = [], []
        for r in range(tiles):
            start = pl.multiple_of(jnp.maximum(first + r - i, 0) * tq, tq)
            zs.append(_dot_nt(qqs[r], k_ref[0, pl.ds(start, tq), :]))
            vs.append(v_ref[0, pl.ds(start, tq), :])
        z = jnp.concatenate(zs, axis=0)
        log_keep = -(jnp.maximum(z, 0.0) + jnp.log2(1.0 + jnp.exp2(-jnp.abs(z))))
        if masked:
            log_keep = jnp.where(strict, log_keep, 0.0)
        csum = _dot(log_keep.astype(BF16), suffix)
        run_in = run if masked else jnp.where(tile_of_row >= i - first, run, -1e30)
        a = jnp.exp2(z + csum + run_in)
        if masked:
            a = jnp.where(strict, a, 0.0)
        a = a.astype(BF16)
        pv = [_dot(a[r * 2 * tq:(r + 1) * 2 * tq], vs[r]) for r in range(tiles)]
        return run_in + csum[:, 0:1], acc + jnp.concatenate(pv, axis=0)

    run, acc = step(0, jnp.zeros((rows, 1), F32), jnp.zeros((rows, LANES), F32), True)
    run, acc = step(1, run, acc, False)

    def cond(st):
        i, run, _ = st
        return jnp.logical_and(first + tiles - 1 - i >= 0, jnp.max(run) > F32_EXP2_ZERO)

    def body(st):
        i, run, acc = st
        run, acc = step(i, run, acc, False)
        return i + 1, run, acc

    _, _, acc = lax.while_loop(cond, body, (jnp.int32(2), run, acc))
    for r in range(tiles):
        o_ref[0, r * tq:(r + 1) * tq, :] = jnp.where(
            lane < HEAD_DIM, acc[2 * r * tq:(2 * r + 1) * tq], acc[(2 * r + 1) * tq:(2 * r + 2) * tq]).astype(o_ref.dtype)


def _sb_attn(proj, tq, tiles):
    b, s, _ = proj.shape
    pairs = SB_HEADS * HEAD_DIM // LANES
    base = 3 * DIFF_HEADS * 2 * HEAD_DIM // LANES
    kern = functools.partial(_sb_attn_kernel, tq=tq, tiles=tiles)
    rows = tq * tiles
    return pl.pallas_call(
        kern,
        grid=(b, pairs, s // rows),
        in_specs=[
            pl.BlockSpec((1, rows, LANES), lambda bi, hi, qi: (bi, qi, base + hi)),
            pl.BlockSpec((1, s, LANES), lambda bi, hi, qi: (bi, 0, base + pairs + hi)),
            pl.BlockSpec((1, s, LANES), lambda bi, hi, qi: (bi, 0, base + 2 * pairs + hi)),
        ],
        out_specs=pl.BlockSpec((1, rows, LANES), lambda bi, hi, qi: (bi, qi, hi)),
        out_shape=jax.ShapeDtypeStruct((b, s, pairs * LANES), BF16),
        compiler_params=pltpu.CompilerParams(
            dimension_semantics=("arbitrary", "arbitrary", "arbitrary"), vmem_limit_bytes=VMEM_LIMIT),
        name="sb_attn",
    )(proj, proj, proj)


def _mix_ffn_kernel(*refs, n_y, sub, final, ffn_width, n_steps):
    h_ref = refs[0]
    y_refs = refs[1:1 + n_y]
    wo_refs = refs[1 + n_y:1 + 2 * n_y]
    fg_ref, wg_ref, wu_ref, wd_ref = refs[1 + 2 * n_y:5 + 2 * n_y]
    fin_ref = refs[5 + 2 * n_y] if final else None
    o_ref, n_ref, acc_ref = refs[-3:]
    j = pl.program_id(1)

    @pl.when(j == 0)
    def _():
        h1 = h_ref[...]
        for y_ref, wo_ref in zip(y_refs, wo_refs):
            h1 = h1 + _dot(y_ref[...], wo_ref[...])
        n_ref[...] = _rms(h1, fg_ref[...], NORM_EPS).astype(BF16)
        acc_ref[...] = h1

    n = n_ref[...]
    fc = wg_ref.shape[1]
    last_j = pl.num_programs(1) - 1

    def chunk(lo):
        g = _dot(n, wg_ref[:, lo:lo + sub])
        u = _dot(n, wu_ref[:, lo:lo + sub])
        return _dot((_silu(g) * u).astype(BF16), wd_ref[lo:lo + sub, :])

    chunks = list(range(0, fc, sub))
    beyond = [lo for lo in chunks if (n_steps - 1) * fc + lo + sub > ffn_width]
    for lo in beyond:
        @pl.when(j < last_j)
        def _(lo=lo):
            acc_ref[...] += chunk(lo)

    acc = acc_ref[...]
    for lo in chunks:
        if lo not in beyond:
            acc = acc + chunk(lo)
    acc_ref[...] = acc

    @pl.when(j == last_j)
    def _():
        out = acc_ref[...]
        if final:
            out = _rms(out, fin_ref[...], NORM_EPS)
        o_ref[...] = out


def _ffn_tiles(f):
    sub = MXU_COLS
    assert f % sub == 0, f
    n_sub = f // sub
    return -(-n_sub // 2) * sub, sub


def _mix_ffn(h2d, ys, wos, fg, wg, wu, wd, layer, fin, tm, fc, sub):
    m, d = h2d.shape
    f = wg.shape[2]
    n_steps = -(-f // fc)
    final = fin is not None
    kern = functools.partial(_mix_ffn_kernel, n_y=len(ys), sub=sub, final=final, ffn_width=f, n_steps=n_steps)
    const = lambda shape: pl.BlockSpec(shape, lambda i, j: (0,) * len(shape))
    in_specs = [pl.BlockSpec((tm, d), lambda i, j: (i, 0))]
    in_specs += [pl.BlockSpec((tm, y.shape[1]), lambda i, j: (i, 0)) for y in ys]
    in_specs += [pl.BlockSpec((y.shape[1], d), lambda i, j, k=k: (k, 0)) for k, y in enumerate(ys)]
    in_specs += [const(fg.shape),
                 pl.BlockSpec((None, d, fc), lambda i, j: (layer, 0, j)),
                 pl.BlockSpec((None, d, fc), lambda i, j: (layer, 0, j)),
                 pl.BlockSpec((None, fc, d), lambda i, j: (layer, j, 0))]
    args = [h2d, *ys, *wos, fg, wg, wu, wd]
    if final:
        in_specs.append(const(fin.shape))
        args.append(fin)
    return pl.pallas_call(
        kern,
        grid=(m // tm, n_steps),
        in_specs=in_specs,
        out_specs=pl.BlockSpec((tm, d), lambda i, j: (i, 0)),
        out_shape=jax.ShapeDtypeStruct((m, d), F32),
        scratch_shapes=[pltpu.VMEM((tm, d), BF16), pltpu.VMEM((tm, d), F32)],
        compiler_params=pltpu.CompilerParams(
            dimension_semantics=("arbitrary", "arbitrary"), vmem_limit_bytes=VMEM_LIMIT),
        name="mix_ffn",
    )(*args)


def _ssd_kernel(zx_ref, dt_ref, dtb_ref, alog_ref, dsk_ref, gn_ref, exp_ref, o_ref, state, *, chunk):
    L = chunk
    gw = SSD_HEADS * SSD_HEAD_DIM // SSD_GROUPS

    @pl.when(pl.program_id(1) == 0)
    def _():
        state[...] = jnp.zeros(state.shape, F32)

    lane = lax.broadcasted_iota(jnp.int32, (1, LANES), 1)
    a = jnp.where(lane < SSD_HEADS, -jnp.exp(alog_ref[...]), 0.0)
    row = lax.broadcasted_iota(jnp.int32, (L, L), 0)
    col = lax.broadcasted_iota(jnp.int32, (L, L), 1)
    lower = row >= col
    tril = jnp.where(lower, 1.0, 0.0).astype(BF16)
    e = exp_ref[...]
    glane = lax.broadcasted_iota(jnp.int32, (1, gw), 1)
    for r0 in range(0, zx_ref.shape[1], L):
        _ssd_chunk(zx_ref, dt_ref, dtb_ref, dsk_ref, gn_ref, o_ref, state, r0, L, a, lower, tril, e, glane)


def _ssd_chunk(zx_ref, dt_ref, dtb_ref, dsk_ref, gn_ref, o_ref, state, r0, L, a, lower, tril, e, glane):
    d_inner = SSD_HEADS * SSD_HEAD_DIM
    gw = d_inner // SSD_GROUPS
    hpg = SSD_HEADS // SSD_GROUPS
    x_off, b_off, c_off = d_inner, 2 * d_inner, 2 * d_inner + SSD_GROUPS * SSD_STATE
    rows = slice(r0, r0 + L)
    dt = _softplus(dt_ref[0, rows, :] + dtb_ref[...])
    d_hi, d_mid, d_lo = _split3(dt * a)
    acum = _dot(tril, d_hi) + _dot(tril, d_mid) + _dot(tril, d_lo)
    last = acum[L - 1:L, :]
    ea = jnp.exp(acum)
    w = jnp.exp(last - acum) * dt
    acum_t = acum.T
    dt_t = dt.T
    w_t = w.T
    c_hi, c_mid, c_lo = _split3(jnp.broadcast_to(jnp.exp(last), (SUBLANES, LANES)))
    cd_full = (_dot(c_hi, e) + _dot(c_mid, e) + _dot(c_lo, e))[0:1, :]

    for g in range(SSD_GROUPS):
        xg_bf = zx_ref[0, rows, x_off + g * gw:x_off + (g + 1) * gw]
        xg = xg_bf.astype(F32)
        bg_bf = zx_ref[0, rows, b_off + g * SSD_STATE:b_off + (g + 1) * SSD_STATE]
        cg_bf = zx_ref[0, rows, c_off + g * SSD_STATE:c_off + (g + 1) * SSD_STATE]
        cg = cg_bf.astype(F32)
        cb = _dot_nt(cg_bf, bg_bf)
        bg_t = bg_bf.astype(F32).T
        prev = state[g]
        prev_bf = prev.astype(BF16)
        y_lhs, y_rhs, st_lhs, st_rhs = [], [], [], []
        for j in range(hpg):
            h = g * hpg + j
            in_head = jnp.logical_and(glane >= j * SSD_HEAD_DIM, glane < (j + 1) * SSD_HEAD_DIM)
            x_h = jnp.where(in_head, xg_bf, jnp.zeros_like(xg_bf))
            prev_h = jnp.where(in_head, prev_bf, jnp.zeros_like(prev_bf))
            seg = acum[:, h:h + 1] - acum_t[h:h + 1, :]
            decay = jnp.exp(jnp.where(lower, seg, -jnp.inf))
            y_lhs += [(cb * decay * dt_t[h:h + 1, :]).astype(BF16), (cg * ea[:, h:h + 1]).astype(BF16)]
            y_rhs += [x_h, prev_h]
            st_lhs.append((bg_t * w_t[h:h + 1, :]).astype(BF16))
            st_rhs.append(x_h)
        y = dsk_ref[:, g * gw:(g + 1) * gw] * xg + _dot(jnp.concatenate(y_lhs, axis=1), jnp.concatenate(y_rhs, axis=0))
        state[g] = prev * cd_full[:, g * gw:(g + 1) * gw] + _dot(
            jnp.concatenate(st_lhs, axis=1), jnp.concatenate(st_rhs, axis=0))
        gated = y * zx_ref[0, rows, g * gw:(g + 1) * gw].astype(F32)
        o_ref[0, rows, g * gw:(g + 1) * gw] = _rms(
            gated, gn_ref[:, g * gw:(g + 1) * gw], SUBLN_EPS).astype(o_ref.dtype)


def _ssd_scan(zx, dt_raw, dt_bias, a_log, d_full, gnorm, expand, chunk, chunks_per_step):
    b, s, width = zx.shape
    d_inner = SSD_HEADS * SSD_HEAD_DIM
    rows = chunk * chunks_per_step
    kern = functools.partial(_ssd_kernel, chunk=chunk)
    const = lambda shape: pl.BlockSpec(shape, lambda bi, ci: (0,) * len(shape))
    return pl.pallas_call(
        kern,
        grid=(b, s // rows),
        in_specs=[
            pl.BlockSpec((1, rows, width), lambda bi, ci: (bi, ci, 0)),
            pl.BlockSpec((1, rows, LANES), lambda bi, ci: (bi, ci, 0)),
            const(dt_bias.shape), const(a_log.shape),
            const(d_full.shape), const(gnorm.shape), const(expand.shape),
        ],
        out_specs=pl.BlockSpec((1, rows, d_inner), lambda bi, ci: (bi, ci, 0)),
        out_shape=jax.ShapeDtypeStruct((b, s, d_inner), BF16),
        scratch_shapes=[pltpu.VMEM((SSD_GROUPS, SSD_STATE, d_inner // SSD_GROUPS), F32)],
        compiler_params=pltpu.CompilerParams(
            dimension_semantics=("arbitrary", "arbitrary"), vmem_limit_bytes=VMEM_LIMIT),
        name="ssd_scan",
    )(zx, dt_raw, dt_bias, a_log, d_full, gnorm, expand)


def _rope_tables(seq):
    half = HEAD_DIM // 2
    inv_freq = ROPE_THETA ** (-jnp.arange(half, dtype=F32) / half)
    ang = jnp.arange(seq, dtype=F32)[:, None] * inv_freq[None, :]
    cos, sin = jnp.cos(ang), jnp.sin(ang)
    reps = LANES // HEAD_DIM
    return jnp.tile(jnp.concatenate([cos, cos], axis=1), (1, reps)), jnp.tile(jnp.concatenate([-sin, sin], axis=1), (1, reps))


def _pad_lanes(v):
    return jnp.pad(v, (0, LANES - v.shape[0]))[None, :]


def _attn_layer(h2d, b, s, norm_w, w_in, lq1, lk1, lq2, lk2, subln, w_out, ffn, lambda_init,
                tm, ffn_tm, tq_diff, tq_sb, sb_tiles):
    cos, sin = _rope_tables(s)
    proj = _attn_proj(h2d, norm_w[None, :], w_in.astype(BF16), cos, sin, s, ffn_tm)
    proj = proj.reshape(b, s, -1)
    lqk = jnp.stack([lq1, lk1, lq2, lk2])
    o_diff = _diff_attn(proj, lqk, subln[None, :], lambda_init, tq_diff)
    o_sb = _sb_attn(proj, tq_sb, sb_tiles)
    assert o_diff.shape[-1] == o_sb.shape[-1]
    w_out_bf = w_out.astype(BF16)
    return _mix_ffn(h2d, [o_diff.reshape(b * s, -1), o_sb.reshape(b * s, -1)], [w_out_bf, w_out_bf], *ffn, ffn_tm,
                    *_ffn_tiles(ffn[1].shape[2]))


def _ssd_layer(h2d, b, s, norm_w, w_in, conv_w, conv_b, dt_bias, a_log, d_skip, gnorm, w_out, ffn, tm, chunk):
    d_inner = SSD_HEADS * SSD_HEAD_DIM
    main = w_in.shape[1] - SSD_HEADS
    w_bf = w_in.astype(BF16)
    w_dt = jnp.pad(w_bf[:, main:], ((0, 0), (0, LANES - SSD_HEADS)))
    zx, dt_raw = _ssd_proj(h2d, norm_w[None, :], w_bf, w_dt, conv_w, conv_b[None, :], s, tm)
    expand = (jnp.arange(LANES)[:, None] == (jnp.arange(d_inner)[None, :] // SSD_HEAD_DIM)).astype(BF16)
    y = _ssd_scan(zx.reshape(b, s, main), dt_raw.reshape(b, s, LANES),
                  _pad_lanes(dt_bias), _pad_lanes(a_log), jnp.repeat(d_skip, SSD_HEAD_DIM)[None, :],
                  gnorm[None, :], expand, chunk, 4 if s % (4 * chunk) == 0 else 1)
    return _mix_ffn(h2d, [y.reshape(b * s, d_inner)], [w_out.astype(BF16)], *ffn, tm, *_ffn_tiles(ffn[1].shape[2]))


def kernel(x, attn_norm, attn_w_in, diff_lq1, diff_lk1, diff_lq2, diff_lk2, diff_subln, attn_w_out, ssd_norm, ssd_w_in, ssd_conv_w, ssd_conv_b, ssd_dt_bias, ssd_a_log, ssd_d, ssd_gnorm, ssd_w_out, ffn_norm, ffn_w_gate, ffn_w_up, ffn_w_down, final_norm):
    b, s, d = x.shape
    depth = ffn_norm.shape[0]
    tm = min(512, s)
    h = x.reshape(b * s, d)
    wg, wu, wd = ffn_w_gate.astype(BF16), ffn_w_up.astype(BF16), ffn_w_down.astype(BF16)
    for layer in range(depth):
        i = layer // 2
        fin = final_norm[None, :] if layer == depth - 1 else None
        ffn = (ffn_norm[layer][None, :], wg, wu, wd, layer, fin)
        if layer % 2 == 0:
            lambda_init = 0.8 - 0.6 * math.exp(-0.3 * layer)
            h = _attn_layer(h, b, s, attn_norm[i], attn_w_in[i], diff_lq1[i], diff_lk1[i], diff_lq2[i],
                            diff_lk2[i], diff_subln[i], attn_w_out[i], ffn, lambda_init,
                            tm=tm, ffn_tm=min(1024, s), tq_diff=min(512, s), tq_sb=min(256, s), sb_tiles=4 if s % 1024 == 0 else 1)
        else:
            h = _ssd_layer(h, b, s, ssd_norm[i], ssd_w_in[i], ssd_conv_w[i], ssd_conv_b[i], ssd_dt_bias[i],
                           ssd_a_log[i], ssd_d[i], ssd_gnorm[i], ssd_w_out[i], ffn,
                           tm=tm, chunk=128)
    return h.reshape(b, s, d)
```

```python
import functools
import math

import jax
import jax.numpy as jnp
from jax import lax
from jax.experimental import pallas as pl
from jax.experimental.pallas import tpu as pltpu

F32 = jnp.float32
BF16 = jnp.bfloat16

HEAD_DIM = 64
DIFF_HEADS = 4
SB_HEADS = 8
ROPE_THETA = 10000.0
NORM_EPS = 1e-6
SUBLN_EPS = 1e-5
SSD_HEADS = 32
SSD_GROUPS = 8
SSD_STATE = 128
SSD_CONV = 4
SSD_HEAD_DIM = 64
LANES = 128
SUBLANES = 8
VMEM_LIMIT = 58 * 1024 * 1024

LOG2E = 1.4426950408889634
QUERY_SCALE = LOG2E * HEAD_DIM ** -0.5
F32_EXP2_ZERO = -160.0


def _rms(x, g, eps):
    return x * lax.rsqrt(jnp.mean(x * x, axis=-1, keepdims=True) + eps) * g


def _silu(x):
    h = 0.5 * x
    return h + h * jnp.tanh(h)


def _softplus(x):
    return jnp.maximum(x, 0.0) + jnp.log1p(jnp.exp(-jnp.abs(x)))


def _split3(x):
    hi = x.astype(BF16)
    r = x - hi.astype(F32)
    mid = r.astype(BF16)
    lo = (r - mid.astype(F32)).astype(BF16)
    return hi, mid, lo


def _dot(a, b):
    return jnp.dot(a, b, preferred_element_type=F32)


def _dot_nt(a, b):
    return lax.dot_general(a, b, (((1,), (1,)), ((), ())), preferred_element_type=F32)


MXU_COLS = 256


def _attn_proj_kernel(x_ref, g_ref, w_ref, cos_ref, sin_ref, o_ref, *, rope_cols, query_cols):
    xn = _rms(x_ref[...], g_ref[...], NORM_EPS).astype(BF16)
    tm = xn.shape[0]
    cos = cos_ref[...]
    sin = sin_ref[...]
    lane = lax.broadcasted_iota(jnp.int32, (tm, LANES), 1)
    first_half = (lane & (HEAD_DIM // 2)) == 0
    for c in range(w_ref.shape[1] // MXU_COLS):
        lo = c * MXU_COLS
        acc = _dot(xn, w_ref[:, lo:lo + MXU_COLS])
        is_query = any(a <= lo < b for a, b in query_cols)
        if lo < rope_cols:
            for s in range(MXU_COLS // LANES):
                t = acc[:, s * LANES:(s + 1) * LANES]
                partner = jnp.where(first_half,
                                    pltpu.roll(t, LANES - HEAD_DIM // 2, 1),
                                    pltpu.roll(t, HEAD_DIM // 2, 1))
                r = t * cos + partner * sin
                if is_query:
                    r = r * QUERY_SCALE
                o_ref[:, lo + s * LANES:lo + (s + 1) * LANES] = r.astype(o_ref.dtype)
        else:
            if is_query:
                acc = acc * QUERY_SCALE
            o_ref[:, lo:lo + MXU_COLS] = acc.astype(o_ref.dtype)


def _attn_proj(x2d, g, w_bf, cos, sin, seq, tm):
    m, d = x2d.shape
    n = w_bf.shape[1]
    diff_w = DIFF_HEADS * 2 * HEAD_DIM
    sb_w = SB_HEADS * HEAD_DIM
    rope_cols = 2 * diff_w
    query_cols = ((0, diff_w), (3 * diff_w, 3 * diff_w + sb_w))
    pos_blocks = seq // tm
    return pl.pallas_call(
        functools.partial(_attn_proj_kernel, rope_cols=rope_cols, query_cols=query_cols),
        grid=(m // tm,),
        in_specs=[
            pl.BlockSpec((tm, d), lambda i: (i, 0)),
            pl.BlockSpec((1, d), lambda i: (0, 0)),
            pl.BlockSpec((d, n), lambda i: (0, 0)),
            pl.BlockSpec((tm, LANES), lambda i: (i % pos_blocks, 0)),
            pl.BlockSpec((tm, LANES), lambda i: (i % pos_blocks, 0)),
        ],
        out_specs=pl.BlockSpec((tm, n), lambda i: (i, 0)),
        out_shape=jax.ShapeDtypeStruct((m, n), BF16),
        compiler_params=pltpu.CompilerParams(
            dimension_semantics=("arbitrary",), vmem_limit_bytes=VMEM_LIMIT),
        name="attn_proj",
    )(x2d, g, w_bf, cos, sin)


def _ssd_proj_kernel(x_ref, g_ref, w_ref, wdt_ref, cw_ref, cb_ref, o_ref, dt_ref, tail_ref, *, z_cols, pos_blocks):
    xn = _rms(x_ref[...], g_ref[...], NORM_EPS).astype(BF16)
    dt_ref[...] = _dot(xn, wdt_ref[...])
    tm = xn.shape[0]
    row8 = lax.broadcasted_iota(jnp.int32, (SUBLANES, MXU_COLS), 0)

    @pl.when(pl.program_id(0) % pos_blocks == 0)
    def _():
        tail_ref[...] = jnp.zeros(tail_ref.shape, F32)

    n_chunks = o_ref.shape[1] // MXU_COLS
    for c in range(n_chunks):
        lo = c * MXU_COLS
        acc = _dot(xn, w_ref[:, lo:lo + MXU_COLS])
        if lo < z_cols:
            out = _silu(acc)
        else:
            cl = lo - z_cols
            tail = tail_ref[:, cl:cl + MXU_COLS]
            tail_ref[:, cl:cl + MXU_COLS] = acc[tm - SUBLANES:, :]
            w0, w1, w2, w3 = (cw_ref[k:k + 1, cl:cl + MXU_COLS] for k in range(SSD_CONV))

            def shift(a, before, d):
                rolled = pltpu.roll(a, d, 0)
                top = jnp.where(row8 < d, pltpu.roll(before, d, 0), rolled[:SUBLANES])
                return jnp.concatenate([top, rolled[SUBLANES:]], axis=0)

            x1 = shift(acc, tail, 1)
            tail1 = pltpu.roll(tail, 1, 0)
            far = w1 * acc + w0 * x1
            far_tail = w1 * tail + w0 * tail1
            conv = cb_ref[:, cl:cl + MXU_COLS] + w3 * acc + w2 * x1 + shift(far, far_tail, 2)
            out = _silu(conv)
        o_ref[:, lo:lo + MXU_COLS] = out.astype(o_ref.dtype)


def _ssd_proj(x2d, g, w_bf, wdt_bf, conv_w, conv_b, seq, tm):
    m, d = x2d.shape
    conv_dim = conv_w.shape[1]
    n = SSD_HEADS * SSD_HEAD_DIM + conv_dim
    assert n % MXU_COLS == 0 and n <= w_bf.shape[1]
    kern = functools.partial(_ssd_proj_kernel, z_cols=n - conv_dim, pos_blocks=seq // tm)
    return pl.pallas_call(
        kern,
        grid=(m // tm,),
        in_specs=[
            pl.BlockSpec((tm, d), lambda i: (i, 0)),
            pl.BlockSpec((1, d), lambda i: (0, 0)),
            pl.BlockSpec(w_bf.shape, lambda i: (0, 0)),
            pl.BlockSpec((d, LANES), lambda i: (0, 0)),
            pl.BlockSpec(conv_w.shape, lambda i: (0, 0)),
            pl.BlockSpec(conv_b.shape, lambda i: (0, 0)),
        ],
        out_specs=[
            pl.BlockSpec((tm, n), lambda i: (i, 0)),
            pl.BlockSpec((tm, LANES), lambda i: (i, 0)),
        ],
        out_shape=[jax.ShapeDtypeStruct((m, n), BF16), jax.ShapeDtypeStruct((m, LANES), F32)],
        scratch_shapes=[pltpu.VMEM((SUBLANES, conv_dim), F32)],
        compiler_params=pltpu.CompilerParams(
            dimension_semantics=("arbitrary",), vmem_limit_bytes=VMEM_LIMIT),
        name="ssd_proj",
    )(x2d, g, w_bf, wdt_bf, conv_w, conv_b)


def _diff_attn_kernel(lqk_ref, q_ref, k_ref, v_ref, subln_ref, o_ref, vt_ref, s0_ref, s1_ref, m_ref, acc_ref, *,
                      tq, lambda_init):
    qi = pl.program_id(2)
    seq = k_ref.shape[1]
    gq = MXU_COLS
    halves = tq // gq

    @pl.when(qi == 0)
    def _():
        for c in range(seq // tq):
            vt_ref[:LANES, c * tq:(c + 1) * tq] = v_ref[0, c * tq:(c + 1) * tq, :].T
        vt_ref[LANES:, :] = jnp.ones((vt_ref.shape[0] - LANES, seq), BF16)

    q = q_ref[0]
    lane = lax.broadcasted_iota(jnp.int32, (tq, LANES), 1)
    zero = jnp.zeros_like(q)
    qq = jnp.concatenate([jnp.where(lane < HEAD_DIM, q, zero), jnp.where(lane >= HEAD_DIM, q, zero)], axis=0)
    nq = 2 * tq
    key_i = lax.broadcasted_iota(jnp.int32, (gq, nq), 0)
    qry_i = lax.broadcasted_iota(jnp.int32, (gq, nq), 1) & (tq - 1)

    s_refs = (s0_ref, s1_ref)

    def scores(kb, slot):
        start = pl.multiple_of(kb * gq, gq)
        s_refs[slot][...] = _dot_nt(k_ref[0, pl.ds(start, gq), :], qq)

    def consume(kb, slot, diag=None):
        start = pl.multiple_of(kb * gq, gq)
        s = s_refs[slot][...]
        if diag is not None:
            s = jnp.where(key_i + diag * gq <= qry_i, s, -jnp.inf)
        m = m_ref[...]
        m_new = jnp.maximum(m, jnp.max(s, axis=0, keepdims=True))
        p = jnp.exp2(s - m_new).astype(BF16)
        acc_ref[...] = jnp.exp2(m - m_new) * acc_ref[...] + _dot(vt_ref[:, pl.ds(start, gq)], p)
        m_ref[...] = m_new

    def pair(kb):
        scores(kb + 1, 1)
        consume(kb, 0)
        scores(kb + 2, 0)
        consume(kb + 1, 1)

    def pairs(first_pair, count):
        for c in range(count):
            pair(2 * (first_pair + c))

    unroll = 4

    def body(j, carry):
        pairs(unroll * j, unroll)
        return carry

    n_full = qi * halves
    m_ref[...] = jnp.full(m_ref.shape, -jnp.inf, F32)
    acc_ref[...] = jnp.zeros(acc_ref.shape, F32)
    scores(0, 0)
    lax.fori_loop(0, qi // unroll, body, 0)
    done = (qi // unroll) * unroll
    for bit in (2, 1):
        @pl.when((qi & bit) != 0)
        def _(bit=bit, done=done):
            pairs(done, bit)
        done = done + (qi & bit)

    scores(n_full + 1, 1)
    consume(n_full, 0, diag=0)
    consume(n_full + 1, 1, diag=1)

    lqk = lqk_ref[...]
    lam = (jnp.exp(jnp.sum(lqk[0:1] * lqk[1:2], axis=-1, keepdims=True))
           - jnp.exp(jnp.sum(lqk[2:3] * lqk[3:4], axis=-1, keepdims=True)) + lambda_init)
    acc = acc_ref[...]
    o_t = acc[:LANES] / acc[LANES:LANES + 1]
    o_t = o_t[:, :tq] - lam * o_t[:, tq:]
    o = _rms(o_t.T, subln_ref[...], SUBLN_EPS) * (1.0 - lambda_init)
    o_ref[0] = o.astype(o_ref.dtype)


def _diff_attn(proj, lqk, subln, lambda_init, tq):
    b, s, _ = proj.shape
    h = DIFF_HEADS
    assert tq == 2 * MXU_COLS and s % tq == 0, (tq, s)
    kern = functools.partial(_diff_attn_kernel, tq=tq, lambda_init=lambda_init)
    return pl.pallas_call(
        kern,
        grid=(b, h, s // tq),
        in_specs=[
            pl.BlockSpec((4, HEAD_DIM), lambda bi, hi, qi: (0, 0)),
            pl.BlockSpec((1, tq, LANES), lambda bi, hi, qi: (bi, qi, hi)),
            pl.BlockSpec((1, s, LANES), lambda bi, hi, qi: (bi, 0, h + hi)),
            pl.BlockSpec((1, s, LANES), lambda bi, hi, qi: (bi, 0, 2 * h + hi)),
            pl.BlockSpec((1, LANES), lambda bi, hi, qi: (0, 0)),
        ],
        out_specs=pl.BlockSpec((1, tq, LANES), lambda bi, hi, qi: (bi, qi, hi)),
        out_shape=jax.ShapeDtypeStruct((b, s, h * LANES), BF16),
        scratch_shapes=[pltpu.VMEM((LANES + 2 * SUBLANES, s), BF16),
                        pltpu.VMEM((MXU_COLS, 2 * tq), F32), pltpu.VMEM((MXU_COLS, 2 * tq), F32),
                        pltpu.VMEM((1, 2 * tq), F32), pltpu.VMEM((LANES + 2 * SUBLANES, 2 * tq), F32)],
        compiler_params=pltpu.CompilerParams(
            dimension_semantics=("arbitrary", "arbitrary", "arbitrary"), vmem_limit_bytes=VMEM_LIMIT),
        name="diff_attn",
    )(lqk, proj, proj, proj, subln)


def _sb_attn_kernel(q_ref, k_ref, v_ref, o_ref, *, tq, tiles):
    first = pl.program_id(2) * tiles
    rows = 2 * tiles * tq
    lane = lax.broadcasted_iota(jnp.int32, (tq, LANES), 1)
    row = lax.broadcasted_iota(jnp.int32, (rows, tq), 0)
    col = lax.broadcasted_iota(jnp.int32, (rows, tq), 1)
    strict = col < (row & (tq - 1))
    suffix = jnp.where(lax.broadcasted_iota(jnp.int32, (tq, tq), 1) <= lax.broadcasted_iota(jnp.int32, (tq, tq), 0),
                       1.0, 0.0).astype(BF16)
    tile_of_row = lax.broadcasted_iota(jnp.int32, (rows, 1), 0) // (2 * tq)

    qqs = []
    for r in range(tiles):
        q = q_ref[0, r * tq:(r + 1) * tq, :]
        zero = jnp.zeros_like(q)
        qqs.append(jnp.concatenate([jnp.where(lane < HEAD_DIM, q, zero), jnp.where(lane >= HEAD_DIM, q, zero)], axis=0))

    def step(i, run, acc, masked):
        vs, zs = [], []
        for r in range(tiles):
            start = pl.multiple_of(jnp.maximum(first + r - i, 0) * tq, tq)
            zs.append(_dot_nt(qqs[r], k_ref[0, pl.ds(start, tq), :]))
            vs.append(v_ref[0, pl.ds(start, tq), :])
        z = jnp.concatenate(zs, axis=0)
        log_keep = -(jnp.maximum(z, 0.0) + jnp.log2(1.0 + jnp.exp2(-jnp.abs(z))))
        if masked:
            log_keep = jnp.where(strict, log_keep, 0.0)
        csum = _dot(log_keep.astype(BF16), suffix)
        run_in = run if masked else jnp.where(tile_of_row >= i - first, run, -1e30)
        a = jnp.exp2(z + csum + run_in)
        if masked:
            a = jnp.where(strict, a, 0.0)
        a = a.astype(BF16)
        pv = [_dot(a[r * 2 * tq:(r + 1) * 2 * tq], vs[r]) for r in range(tiles)]
        return run_in + csum[:, 0:1], acc + jnp.concatenate(pv, axis=0)

    run, acc = step(0, jnp.zeros((rows, 1), F32), jnp.zeros((rows, LANES), F32), True)
    run, acc = step(1, run, acc, False)

    def cond(st):
        i, run, _ = st
        return jnp.logical_and(first + tiles - 1 - i >= 0, jnp.max(run) > F32_EXP2_ZERO)

    def body(st):
        i, run, acc = st
        run, acc = step(i, run, acc, False)
        return i + 1, run, acc

    _, _, acc = lax.while_loop(cond, body, (jnp.int32(2), run, acc))
    for r in range(tiles):
        o_ref[0, r * tq:(r + 1) * tq, :] = jnp.where(
            lane < HEAD_DIM, acc[2 * r * tq:(2 * r + 1) * tq], acc[(2 * r + 1) * tq:(2 * r + 2) * tq]).astype(o_ref.dtype)


def _sb_attn(proj, tq, tiles):
    b, s, _ = proj.shape
    pairs = SB_HEADS * HEAD_DIM // LANES
    base = 3 * DIFF_HEADS * 2 * HEAD_DIM // LANES
    kern = functools.partial(_sb_attn_kernel, tq=tq, tiles=tiles)
    rows = tq * tiles
    return pl.pallas_call(
        kern,
        grid=(b, pairs, s // rows),
        in_specs=[
            pl.BlockSpec((1, rows, LANES), lambda bi, hi, qi: (bi, qi, base + hi)),
            pl.BlockSpec((1, s, LANES), lambda bi, hi, qi: (bi, 0, base + pairs + hi)),
            pl.BlockSpec((1, s, LANES), lambda bi, hi, qi: (bi, 0, base + 2 * pairs + hi)),
        ],
        out_specs=pl.BlockSpec((1, rows, LANES), lambda bi, hi, qi: (bi, qi, hi)),
        out_shape=jax.ShapeDtypeStruct((b, s, pairs * LANES), BF16),
        compiler_params=pltpu.CompilerParams(
            dimension_semantics=("arbitrary", "arbitrary", "arbitrary"), vmem_limit_bytes=VMEM_LIMIT),
        name="sb_attn",
    )(proj, proj, proj)


def _mix_ffn_kernel(*refs, n_y, sub, final, ffn_width, n_steps):
    h_ref = refs[0]
    y_refs = refs[1:1 + n_y]
    wo_refs = refs[1 + n_y:1 + 2 * n_y]
    fg_ref, wg_ref, wu_ref, wd_ref = refs[1 + 2 * n_y:5 + 2 * n_y]
    fin_ref = refs[5 + 2 * n_y] if final else None
    o_ref, n_ref, acc_ref = refs[-3:]
    j = pl.program_id(1)

    @pl.when(j == 0)
    def _():
        h1 = h_ref[...]
        for y_ref, wo_ref in zip(y_refs, wo_refs):
            h1 = h1 + _dot(y_ref[...], wo_ref[...])
        n_ref[...] = _rms(h1, fg_ref[...], NORM_EPS).astype(BF16)
        acc_ref[...] = h1

    n = n_ref[...]
    fc = wg_ref.shape[1]
    last_j = pl.num_programs(1) - 1

    def chunk(lo):
        g = _dot(n, wg_ref[:, lo:lo + sub])
        u = _dot(n, wu_ref[:, lo:lo + sub])
        return _dot((_silu(g) * u).astype(BF16), wd_ref[lo:lo + sub, :])

    chunks = list(range(0, fc, sub))
    beyond = [lo for lo in chunks if (n_steps - 1) * fc + lo + sub > ffn_width]
    for lo in beyond:
        @pl.when(j < last_j)
        def _(lo=lo):
            acc_ref[...] += chunk(lo)

    acc = acc_ref[...]
    for lo in chunks:
        if lo not in beyond:
            acc = acc + chunk(lo)
    acc_ref[...] = acc

    @pl.when(j == last_j)
    def _():
        out = acc_ref[...]
        if final:
            out = _rms(out, fin_ref[...], NORM_EPS)
        o_ref[...] = out


def _ffn_tiles(f, n_steps=2):
    sub = MXU_COLS
    assert f % sub == 0, f
    n_sub = f // sub
    return -(-n_sub // n_steps) * sub, sub


def _mix_ffn(h2d, ys, wos, fg, wg, wu, wd, layer, fin, tm, fc, sub):
    m, d = h2d.shape
    f = wg.shape[2]
    n_steps = -(-f // fc)
    final = fin is not None
    kern = functools.partial(_mix_ffn_kernel, n_y=len(ys), sub=sub, final=final, ffn_width=f, n_steps=n_steps)
    const = lambda shape: pl.BlockSpec(shape, lambda i, j: (0,) * len(shape))
    in_specs = [pl.BlockSpec((tm, d), lambda i, j: (i, 0))]
    in_specs += [pl.BlockSpec((tm, y.shape[1]), lambda i, j: (i, 0)) for y in ys]
    in_specs += [pl.BlockSpec((y.shape[1], d), lambda i, j, k=k: (k, 0)) for k, y in enumerate(ys)]
    in_specs += [const(fg.shape),
                 pl.BlockSpec((None, d, fc), lambda i, j: (layer, 0, j)),
                 pl.BlockSpec((None, d, fc), lambda i, j: (layer, 0, j)),
                 pl.BlockSpec((None, fc, d), lambda i, j: (layer, j, 0))]
    args = [h2d, *ys, *wos, fg, wg, wu, wd]
    if final:
        in_specs.append(const(fin.shape))
        args.append(fin)
    return pl.pallas_call(
        kern,
        grid=(m // tm, n_steps),
        in_specs=in_specs,
        out_specs=pl.BlockSpec((tm, d), lambda i, j: (i, 0)),
        out_shape=jax.ShapeDtypeStruct((m, d), F32),
        scratch_shapes=[pltpu.VMEM((tm, d), BF16), pltpu.VMEM((tm, d), F32)],
        compiler_params=pltpu.CompilerParams(
            dimension_semantics=("arbitrary", "arbitrary"), vmem_limit_bytes=VMEM_LIMIT),
        name="mix_ffn",
    )(*args)


def _ssd_kernel(zx_ref, dt_ref, dtb_ref, alog_ref, dsk_ref, gn_ref, exp_ref, o_ref, state, *, chunk):
    L = chunk
    gw = SSD_HEADS * SSD_HEAD_DIM // SSD_GROUPS

    @pl.when(pl.program_id(1) == 0)
    def _():
        state[...] = jnp.zeros(state.shape, F32)

    lane = lax.broadcasted_iota(jnp.int32, (1, LANES), 1)
    a = jnp.where(lane < SSD_HEADS, -jnp.exp(alog_ref[...]), 0.0)
    row = lax.broadcasted_iota(jnp.int32, (L, L), 0)
    col = lax.broadcasted_iota(jnp.int32, (L, L), 1)
    lower = row >= col
    tril = jnp.where(lower, 1.0, 0.0).astype(BF16)
    e = exp_ref[...]
    glane = lax.broadcasted_iota(jnp.int32, (1, gw), 1)
    for r0 in range(0, zx_ref.shape[1], L):
        _ssd_chunk(zx_ref, dt_ref, dtb_ref, dsk_ref, gn_ref, o_ref, state, r0, L, a, lower, tril, e, glane)


def _ssd_chunk(zx_ref, dt_ref, dtb_ref, dsk_ref, gn_ref, o_ref, state, r0, L, a, lower, tril, e, glane):
    d_inner = SSD_HEADS * SSD_HEAD_DIM
    gw = d_inner // SSD_GROUPS
    hpg = SSD_HEADS // SSD_GROUPS
    x_off, b_off, c_off = d_inner, 2 * d_inner, 2 * d_inner + SSD_GROUPS * SSD_STATE
    rows = slice(r0, r0 + L)
    dt = _softplus(dt_ref[0, rows, :] + dtb_ref[...])
    d_hi, d_mid, d_lo = _split3(dt * a)
    acum = _dot(tril, d_hi) + _dot(tril, d_mid) + _dot(tril, d_lo)
    last = acum[L - 1:L, :]
    ea = jnp.exp(acum)
    w = jnp.exp(last - acum) * dt
    acum_t = acum.T
    dt_t = dt.T
    w_t = w.T
    c_hi, c_mid, c_lo = _split3(jnp.broadcast_to(jnp.exp(last), (SUBLANES, LANES)))
    cd_full = (_dot(c_hi, e) + _dot(c_mid, e) + _dot(c_lo, e))[0:1, :]

    for g in range(SSD_GROUPS):
        xg_bf = zx_ref[0, rows, x_off + g * gw:x_off + (g + 1) * gw]
        xg = xg_bf.astype(F32)
        bg_bf = zx_ref[0, rows, b_off + g * SSD_STATE:b_off + (g + 1) * SSD_STATE]
        cg_bf = zx_ref[0, rows, c_off + g * SSD_STATE:c_off + (g + 1) * SSD_STATE]
        cg = cg_bf.astype(F32)
        cb = _dot_nt(cg_bf, bg_bf)
        bg_t = bg_bf.astype(F32).T
        prev = state[g]
        prev_bf = prev.astype(BF16)
        y_lhs, y_rhs, st_lhs, st_rhs = [], [], [], []
        for j in range(hpg):
            h = g * hpg + j
            in_head = jnp.logical_and(glane >= j * SSD_HEAD_DIM, glane < (j + 1) * SSD_HEAD_DIM)
            x_h = jnp.where(in_head, xg_bf, jnp.zeros_like(xg_bf))
            prev_h = jnp.where(in_head, prev_bf, jnp.zeros_like(prev_bf))
            seg = acum[:, h:h + 1] - acum_t[h:h + 1, :]
            decay = jnp.exp(jnp.where(lower, seg, -jnp.inf))
            y_lhs += [(cb * decay * dt_t[h:h + 1, :]).astype(BF16), (cg * ea[:, h:h + 1]).astype(BF16)]
            y_rhs += [x_h, prev_h]
            st_lhs.append((bg_t * w_t[h:h + 1, :]).astype(BF16))
            st_rhs.append(x_h)
        y = dsk_ref[:, g * gw:(g + 1) * gw] * xg + _dot(jnp.concatenate(y_lhs, axis=1), jnp.concatenate(y_rhs, axis=0))
        state[g] = prev * cd_full[:, g * gw:(g + 1) * gw] + _dot(
            jnp.concatenate(st_lhs, axis=1), jnp.concatenate(st_rhs, axis=0))
        gated = y * zx_ref[0, rows, g * gw:(g + 1) * gw].astype(F32)
        o_ref[0, rows, g * gw:(g + 1) * gw] = _rms(
            gated, gn_ref[:, g * gw:(g + 1) * gw], SUBLN_EPS).astype(o_ref.dtype)


def _ssd_scan(zx, dt_raw, dt_bias, a_log, d_full, gnorm, expand, chunk, chunks_per_step):
    b, s, width = zx.shape
    d_inner = SSD_HEADS * SSD_HEAD_DIM
    rows = chunk * chunks_per_step
    kern = functools.partial(_ssd_kernel, chunk=chunk)
    const = lambda shape: pl.BlockSpec(shape, lambda bi, ci: (0,) * len(shape))
    return pl.pallas_call(
        kern,
        grid=(b, s // rows),
        in_specs=[
            pl.BlockSpec((1, rows, width), lambda bi, ci: (bi, ci, 0)),
            pl.BlockSpec((1, rows, LANES), lambda bi, ci: (bi, ci, 0)),
            const(dt_bias.shape), const(a_log.shape),
            const(d_full.shape), const(gnorm.shape), const(expand.shape),
        ],
        out_specs=pl.BlockSpec((1, rows, d_inner), lambda bi, ci: (bi, ci, 0)),
        out_shape=jax.ShapeDtypeStruct((b, s, d_inner), BF16),
        scratch_shapes=[pltpu.VMEM((SSD_GROUPS, SSD_STATE, d_inner // SSD_GROUPS), F32)],
        compiler_params=pltpu.CompilerParams(
            dimension_semantics=("arbitrary", "arbitrary"), vmem_limit_bytes=VMEM_LIMIT),
        name="ssd_scan",
    )(zx, dt_raw, dt_bias, a_log, d_full, gnorm, expand)


def _rope_tables(seq):
    half = HEAD_DIM // 2
    inv_freq = ROPE_THETA ** (-jnp.arange(half, dtype=F32) / half)
    ang = jnp.arange(seq, dtype=F32)[:, None] * inv_freq[None, :]
    cos, sin = jnp.cos(ang), jnp.sin(ang)
    reps = LANES // HEAD_DIM
    return jnp.tile(jnp.concatenate([cos, cos], axis=1), (1, reps)), jnp.tile(jnp.concatenate([-sin, sin], axis=1), (1, reps))


def _pad_lanes(v):
    return jnp.pad(v, (0, LANES - v.shape[0]))[None, :]


def _attn_layer(h2d, b, s, norm_w, w_in, lq1, lk1, lq2, lk2, subln, w_out, ffn, lambda_init,
                tm, ffn_tm, tq_diff, tq_sb, sb_tiles):
    cos, sin = _rope_tables(s)
    proj = _attn_proj(h2d, norm_w[None, :], w_in.astype(BF16), cos, sin, s, ffn_tm)
    proj = proj.reshape(b, s, -1)
    lqk = jnp.stack([lq1, lk1, lq2, lk2])
    o_diff = _diff_attn(proj, lqk, subln[None, :], lambda_init, tq_diff)
    o_sb = _sb_attn(proj, tq_sb, sb_tiles)
    assert o_diff.shape[-1] == o_sb.shape[-1]
    w_out_bf = w_out.astype(BF16)
    return _mix_ffn(h2d, [o_diff.reshape(b * s, -1), o_sb.reshape(b * s, -1)], [w_out_bf, w_out_bf], *ffn, ffn_tm,
                    *_ffn_tiles(ffn[1].shape[2]))


def _ssd_layer(h2d, b, s, norm_w, w_in, conv_w, conv_b, dt_bias, a_log, d_skip, gnorm, w_out, ffn, tm, ffn_tm, chunk):
    d_inner = SSD_HEADS * SSD_HEAD_DIM
    main = w_in.shape[1] - SSD_HEADS
    w_bf = w_in.astype(BF16)
    w_dt = jnp.pad(w_bf[:, main:], ((0, 0), (0, LANES - SSD_HEADS)))
    zx, dt_raw = _ssd_proj(h2d, norm_w[None, :], w_bf, w_dt, conv_w, conv_b[None, :], s, tm)
    expand = (jnp.arange(LANES)[:, None] == (jnp.arange(d_inner)[None, :] // SSD_HEAD_DIM)).astype(BF16)
    y = _ssd_scan(zx.reshape(b, s, main), dt_raw.reshape(b, s, LANES),
                  _pad_lanes(dt_bias), _pad_lanes(a_log), jnp.repeat(d_skip, SSD_HEAD_DIM)[None, :],
                  gnorm[None, :], expand, chunk, 4 if s % (4 * chunk) == 0 else 1)
    return _mix_ffn(h2d, [y.reshape(b * s, d_inner)], [w_out.astype(BF16)], *ffn, ffn_tm,
                    *_ffn_tiles(ffn[1].shape[2], 3))


def kernel(x, attn_norm, attn_w_in, diff_lq1, diff_lk1, diff_lq2, diff_lk2, diff_subln, attn_w_out, ssd_norm, ssd_w_in, ssd_conv_w, ssd_conv_b, ssd_dt_bias, ssd_a_log, ssd_d, ssd_gnorm, ssd_w_out, ffn_norm, ffn_w_gate, ffn_w_up, ffn_w_down, final_norm):
    b, s, d = x.shape
    depth = ffn_norm.shape[0]
    tm = min(512, s)
    h = x.reshape(b * s, d)
    wg, wu, wd = ffn_w_gate.astype(BF16), ffn_w_up.astype(BF16), ffn_w_down.astype(BF16)
    for layer in range(depth):
        i = layer // 2
        fin = final_norm[None, :] if layer == depth - 1 else None
        ffn = (ffn_norm[layer][None, :], wg, wu, wd, layer, fin)
        if layer % 2 == 0:
            lambda_init = 0.8 - 0.6 * math.exp(-0.3 * layer)
            h = _attn_layer(h, b, s, attn_norm[i], attn_w_in[i], diff_lq1[i], diff_lk1[i], diff_lq2[i],
                            diff_lk2[i], diff_subln[i], attn_w_out[i], ffn, lambda_init,
                            tm=tm, ffn_tm=min(1024, s), tq_diff=min(512, s), tq_sb=min(256, s), sb_tiles=4 if s % 1024 == 0 else 1)
        else:
            h = _ssd_layer(h, b, s, ssd_norm[i], ssd_w_in[i], ssd_conv_w[i], ssd_conv_b[i], ssd_dt_bias[i],
                           ssd_a_log[i], ssd_d[i], ssd_gnorm[i], ssd_w_out[i], ffn,
                           tm=tm, ffn_tm=min(1024, s), chunk=128)
    return h.reshape(b, s, d)
```

```python
import functools
import math

import jax
import jax.numpy as jnp
from jax import lax
from jax.experimental import pallas as pl
from jax.experimental.pallas import tpu as pltpu

F32 = jnp.float32
BF16 = jnp.bfloat16

HEAD_DIM = 64
DIFF_HEADS = 4
SB_HEADS = 8
ROPE_THETA = 10000.0
NORM_EPS = 1e-6
SUBLN_EPS = 1e-5
SSD_HEADS = 32
SSD_GROUPS = 8
SSD_STATE = 128
SSD_CONV = 4
SSD_HEAD_DIM = 64
LANES = 128
SUBLANES = 8
VMEM_LIMIT = 58 * 1024 * 1024

LOG2E = 1.4426950408889634
QUERY_SCALE = LOG2E * HEAD_DIM ** -0.5
F32_EXP2_ZERO = -160.0


def _rms(x, g, eps):
    return x * lax.rsqrt(jnp.mean(x * x, axis=-1, keepdims=True) + eps) * g


def _silu(x):
    h = 0.5 * x
    return h + h * jnp.tanh(h)


def _softplus(x):
    return jnp.maximum(x, 0.0) + jnp.log1p(jnp.exp(-jnp.abs(x)))


def _split3(x):
    hi = x.astype(BF16)
    r = x - hi.astype(F32)
    mid = r.astype(BF16)
    lo = (r - mid.astype(F32)).astype(BF16)
    return hi, mid, lo


def _dot(a, b):
    return jnp.dot(a, b, preferred_element_type=F32)


def _dot_nt(a, b):
    return lax.dot_general(a, b, (((1,), (1,)), ((), ())), preferred_element_type=F32)


MXU_COLS = 256


def _attn_proj_kernel(x_ref, g_ref, w_ref, cos_ref, sin_ref, o_ref, *, rope_cols, query_cols):
    xn = _rms(x_ref[...], g_ref[...], NORM_EPS).astype(BF16)
    tm = xn.shape[0]
    cos = cos_ref[...]
    sin = sin_ref[...]
    lane = lax.broadcasted_iota(jnp.int32, (tm, LANES), 1)
    first_half = (lane & (HEAD_DIM // 2)) == 0
    for c in range(w_ref.shape[1] // MXU_COLS):
        lo = c * MXU_COLS
        acc = _dot(xn, w_ref[:, lo:lo + MXU_COLS])
        is_query = any(a <= lo < b for a, b in query_cols)
        if lo < rope_cols:
            for s in range(MXU_COLS // LANES):
                t = acc[:, s * LANES:(s + 1) * LANES]
                partner = jnp.where(first_half,
                                    pltpu.roll(t, LANES - HEAD_DIM // 2, 1),
                                    pltpu.roll(t, HEAD_DIM // 2, 1))
                r = t * cos + partner * sin
                if is_query:
                    r = r * QUERY_SCALE
                o_ref[:, lo + s * LANES:lo + (s + 1) * LANES] = r.astype(o_ref.dtype)
        else:
            if is_query:
                acc = acc * QUERY_SCALE
            o_ref[:, lo:lo + MXU_COLS] = acc.astype(o_ref.dtype)


def _attn_proj(x2d, g, w_bf, cos, sin, seq, tm):
    m, d = x2d.shape
    n = w_bf.shape[1]
    diff_w = DIFF_HEADS * 2 * HEAD_DIM
    sb_w = SB_HEADS * HEAD_DIM
    rope_cols = 2 * diff_w
    query_cols = ((0, diff_w), (3 * diff_w, 3 * diff_w + sb_w))
    pos_blocks = seq // tm
    return pl.pallas_call(
        functools.partial(_attn_proj_kernel, rope_cols=rope_cols, query_cols=query_cols),
        grid=(m // tm,),
        in_specs=[
            pl.BlockSpec((tm, d), lambda i: (i, 0)),
            pl.BlockSpec((1, d), lambda i: (0, 0)),
            pl.BlockSpec((d, n), lambda i: (0, 0)),
            pl.BlockSpec((tm, LANES), lambda i: (i % pos_blocks, 0)),
            pl.BlockSpec((tm, LANES), lambda i: (i % pos_blocks, 0)),
        ],
        out_specs=pl.BlockSpec((tm, n), lambda i: (i, 0)),
        out_shape=jax.ShapeDtypeStruct((m, n), BF16),
        compiler_params=pltpu.CompilerParams(
            dimension_semantics=("arbitrary",), vmem_limit_bytes=VMEM_LIMIT),
        name="attn_proj",
    )(x2d, g, w_bf, cos, sin)


def _ssd_proj_kernel(x_ref, g_ref, w_ref, wdt_ref, cw_ref, cb_ref, o_ref, dt_ref, tail_ref, *, z_cols, pos_blocks):
    xn = _rms(x_ref[...], g_ref[...], NORM_EPS).astype(BF16)
    dt_ref[...] = _dot(xn, wdt_ref[...])
    tm = xn.shape[0]
    row8 = lax.broadcasted_iota(jnp.int32, (SUBLANES, MXU_COLS), 0)

    @pl.when(pl.program_id(0) % pos_blocks == 0)
    def _():
        tail_ref[...] = jnp.zeros(tail_ref.shape, F32)

    n_chunks = o_ref.shape[1] // MXU_COLS
    for c in range(n_chunks):
        lo = c * MXU_COLS
        acc = _dot(xn, w_ref[:, lo:lo + MXU_COLS])
        if lo < z_cols:
            out = _silu(acc)
        else:
            cl = lo - z_cols
            tail = tail_ref[:, cl:cl + MXU_COLS]
            tail_ref[:, cl:cl + MXU_COLS] = acc[tm - SUBLANES:, :]
            w0, w1, w2, w3 = (cw_ref[k:k + 1, cl:cl + MXU_COLS] for k in range(SSD_CONV))

            def shift(a, before, d):
                rolled = pltpu.roll(a, d, 0)
                top = jnp.where(row8 < d, pltpu.roll(before, d, 0), rolled[:SUBLANES])
                return jnp.concatenate([top, rolled[SUBLANES:]], axis=0)

            x1 = shift(acc, tail, 1)
            tail1 = pltpu.roll(tail, 1, 0)
            far = w1 * acc + w0 * x1
            far_tail = w1 * tail + w0 * tail1
            conv = cb_ref[:, cl:cl + MXU_COLS] + w3 * acc + w2 * x1 + shift(far, far_tail, 2)
            out = _silu(conv)
        o_ref[:, lo:lo + MXU_COLS] = out.astype(o_ref.dtype)


def _ssd_proj(x2d, g, w_bf, wdt_bf, conv_w, conv_b, seq, tm):
    m, d = x2d.shape
    conv_dim = conv_w.shape[1]
    n = SSD_HEADS * SSD_HEAD_DIM + conv_dim
    assert n % MXU_COLS == 0 and n <= w_bf.shape[1]
    kern = functools.partial(_ssd_proj_kernel, z_cols=n - conv_dim, pos_blocks=seq // tm)
    return pl.pallas_call(
        kern,
        grid=(m // tm,),
        in_specs=[
            pl.BlockSpec((tm, d), lambda i: (i, 0)),
            pl.BlockSpec((1, d), lambda i: (0, 0)),
            pl.BlockSpec(w_bf.shape, lambda i: (0, 0)),
            pl.BlockSpec((d, LANES), lambda i: (0, 0)),
            pl.BlockSpec(conv_w.shape, lambda i: (0, 0)),
            pl.BlockSpec(conv_b.shape, lambda i: (0, 0)),
        ],
        out_specs=[
            pl.BlockSpec((tm, n), lambda i: (i, 0)),
            pl.BlockSpec((tm, LANES), lambda i: (i, 0)),
        ],
        out_shape=[jax.ShapeDtypeStruct((m, n), BF16), jax.ShapeDtypeStruct((m, LANES), F32)],
        scratch_shapes=[pltpu.VMEM((SUBLANES, conv_dim), F32)],
        compiler_params=pltpu.CompilerParams(
            dimension_semantics=("arbitrary",), vmem_limit_bytes=VMEM_LIMIT),
        name="ssd_proj",
    )(x2d, g, w_bf, wdt_bf, conv_w, conv_b)


def _diff_attn_kernel(lqk_ref, q_ref, k_ref, v_ref, subln_ref, o_ref, vt_ref, s0_ref, s1_ref, m_ref, acc_ref, *,
                      tq, tiles, lambda_init):
    seq = k_ref.shape[1]

    @pl.when(pl.program_id(2) == 0)
    def _():
        for c in range(seq // tq):
            vt_ref[:LANES, c * tq:(c + 1) * tq] = v_ref[0, c * tq:(c + 1) * tq, :].T
        vt_ref[LANES:, :] = jnp.ones((vt_ref.shape[0] - LANES, seq), BF16)

    for t in range(tiles):
        _diff_tile(pl.program_id(2) * tiles + t, q_ref.at[0, t * tq:(t + 1) * tq, :], o_ref.at[0, t * tq:(t + 1) * tq, :],
                   lqk_ref, k_ref, subln_ref, vt_ref, s0_ref, s1_ref, m_ref, acc_ref, tq, lambda_init)


def _diff_tile(qi, q_view, o_view, lqk_ref, k_ref, subln_ref, vt_ref, s0_ref, s1_ref, m_ref, acc_ref, tq, lambda_init):
    gq = MXU_COLS
    halves = tq // gq
    q = q_view[...]
    lane = lax.broadcasted_iota(jnp.int32, (tq, LANES), 1)
    zero = jnp.zeros_like(q)
    qq = jnp.concatenate([jnp.where(lane < HEAD_DIM, q, zero), jnp.where(lane >= HEAD_DIM, q, zero)], axis=0)
    nq = 2 * tq
    key_i = lax.broadcasted_iota(jnp.int32, (gq, nq), 0)
    qry_i = lax.broadcasted_iota(jnp.int32, (gq, nq), 1) & (tq - 1)

    s_refs = (s0_ref, s1_ref)

    def scores(kb, slot):
        start = pl.multiple_of(kb * gq, gq)
        s_refs[slot][...] = _dot_nt(k_ref[0, pl.ds(start, gq), :], qq)

    def consume(kb, slot, diag=None):
        start = pl.multiple_of(kb * gq, gq)
        s = s_refs[slot][...]
        if diag is not None:
            s = jnp.where(key_i + diag * gq <= qry_i, s, -jnp.inf)
        m = m_ref[...]
        m_new = jnp.maximum(m, jnp.max(s, axis=0, keepdims=True))
        p = jnp.exp2(s - m_new).astype(BF16)
        acc_ref[...] = jnp.exp2(m - m_new) * acc_ref[...] + _dot(vt_ref[:, pl.ds(start, gq)], p)
        m_ref[...] = m_new

    def pair(kb):
        scores(kb + 1, 1)
        consume(kb, 0)
        scores(kb + 2, 0)
        consume(kb + 1, 1)

    def pairs(first_pair, count):
        for c in range(count):
            pair(2 * (first_pair + c))

    unroll = 4

    def body(j, carry):
        pairs(unroll * j, unroll)
        return carry

    n_full = qi * halves
    m_ref[...] = jnp.full(m_ref.shape, -jnp.inf, F32)
    acc_ref[...] = jnp.zeros(acc_ref.shape, F32)
    scores(0, 0)
    lax.fori_loop(0, qi // unroll, body, 0)
    done = (qi // unroll) * unroll
    for bit in (2, 1):
        @pl.when((qi & bit) != 0)
        def _(bit=bit, done=done):
            pairs(done, bit)
        done = done + (qi & bit)

    scores(n_full + 1, 1)
    consume(n_full, 0, diag=0)
    consume(n_full + 1, 1, diag=1)

    lqk = lqk_ref[...]
    lam = (jnp.exp(jnp.sum(lqk[0:1] * lqk[1:2], axis=-1, keepdims=True))
           - jnp.exp(jnp.sum(lqk[2:3] * lqk[3:4], axis=-1, keepdims=True)) + lambda_init)
    acc = acc_ref[...]
    o_t = acc[:LANES] / acc[LANES:LANES + 1]
    o_t = o_t[:, :tq] - lam * o_t[:, tq:]
    o = _rms(o_t.T, subln_ref[...], SUBLN_EPS) * (1.0 - lambda_init)
    o_view[...] = o.astype(o_view.dtype)


def _diff_attn(proj, lqk, subln, lambda_init, tq):
    b, s, _ = proj.shape
    h = DIFF_HEADS
    assert tq == 2 * MXU_COLS and s % tq == 0, (tq, s)
    tiles = 2 if s % (2 * tq) == 0 else 1
    kern = functools.partial(_diff_attn_kernel, tq=tq, tiles=tiles, lambda_init=lambda_init)
    return pl.pallas_call(
        kern,
        grid=(b, h, s // (tq * tiles)),
        in_specs=[
            pl.BlockSpec((4, HEAD_DIM), lambda bi, hi, qi: (0, 0)),
            pl.BlockSpec((1, tq * tiles, LANES), lambda bi, hi, qi: (bi, qi, hi)),
            pl.BlockSpec((1, s, LANES), lambda bi, hi, qi: (bi, 0, h + hi)),
            pl.BlockSpec((1, s, LANES), lambda bi, hi, qi: (bi, 0, 2 * h + hi)),
            pl.BlockSpec((1, LANES), lambda bi, hi, qi: (0, 0)),
        ],
        out_specs=pl.BlockSpec((1, tq * tiles, LANES), lambda bi, hi, qi: (bi, qi, hi)),
        out_shape=jax.ShapeDtypeStruct((b, s, h * LANES), BF16),
        scratch_shapes=[pltpu.VMEM((LANES + 2 * SUBLANES, s), BF16),
                        pltpu.VMEM((MXU_COLS, 2 * tq), F32), pltpu.VMEM((MXU_COLS, 2 * tq), F32),
                        pltpu.VMEM((1, 2 * tq), F32), pltpu.VMEM((LANES + 2 * SUBLANES, 2 * tq), F32)],
        compiler_params=pltpu.CompilerParams(
            dimension_semantics=("arbitrary", "arbitrary", "arbitrary"), vmem_limit_bytes=VMEM_LIMIT),
        name="diff_attn",
    )(lqk, proj, proj, proj, subln)


def _sb_attn_kernel(q_ref, k_ref, v_ref, o_ref, *, tq, tiles):
    first = pl.program_id(2) * tiles
    rows = 2 * tiles * tq
    lane = lax.broadcasted_iota(jnp.int32, (tq, LANES), 1)
    row = lax.broadcasted_iota(jnp.int32, (rows, tq), 0)
    col = lax.broadcasted_iota(jnp.int32, (rows, tq), 1)
    strict = col < (row & (tq - 1))
    suffix = jnp.where(lax.broadcasted_iota(jnp.int32, (tq, tq), 1) <= lax.broadcasted_iota(jnp.int32, (tq, tq), 0),
                       1.0, 0.0).astype(BF16)
    tile_of_row = lax.broadcasted_iota(jnp.int32, (rows, 1), 0) // (2 * tq)

    qqs = []
    for r in range(tiles):
        q = q_ref[0, r * tq:(r + 1) * tq, :]
        zero = jnp.zeros_like(q)
        qqs.append(jnp.concatenate([jnp.where(lane < HEAD_DIM, q, zero), jnp.where(lane >= HEAD_DIM, q, zero)], axis=0))

    def step(i, run, acc, masked):
        vs, zs = [], []
        for r in range(tiles):
            start = pl.multiple_of(jnp.maximum(first + r - i, 0) * tq, tq)
            zs.append(_dot_nt(qqs[r], k_ref[0, pl.ds(start, tq), :]))
            vs.append(v_ref[0, pl.ds(start, tq), :])
        z = jnp.concatenate(zs, axis=0)
        log_keep = -(jnp.maximum(z, 0.0) + jnp.log2(1.0 + jnp.exp2(-jnp.abs(z))))
        if masked:
            log_keep = jnp.where(strict, log_keep, 0.0)
        csum = _dot(log_keep.astype(BF16), suffix)
        run_in = run if masked else jnp.where(tile_of_row >= i - first, run, -1e30)
        a = jnp.exp2(z + csum + run_in)
        if masked:
            a = jnp.where(strict, a, 0.0)
        a = a.astype(BF16)
        pv = [_dot(a[r * 2 * tq:(r + 1) * 2 * tq], vs[r]) for r in range(tiles)]
        return run_in + csum[:, 0:1], acc + jnp.concatenate(pv, axis=0)

    run, acc = step(0, jnp.zeros((rows, 1), F32), jnp.zeros((rows, LANES), F32), True)
    run, acc = step(1, run, acc, False)

    def cond(st):
        i, run, _ = st
        return jnp.logical_and(first + tiles - 1 - i >= 0, jnp.max(run) > F32_EXP2_ZERO)

    def body(st):
        i, run, acc = st
        run, acc = step(i, run, acc, False)
        return i + 1, run, acc

    _, _, acc = lax.while_loop(cond, body, (jnp.int32(2), run, acc))
    for r in range(tiles):
        o_ref[0, r * tq:(r + 1) * tq, :] = jnp.where(
            lane < HEAD_DIM, acc[2 * r * tq:(2 * r + 1) * tq], acc[(2 * r + 1) * tq:(2 * r + 2) * tq]).astype(o_ref.dtype)


def _sb_attn(proj, tq, tiles):
    b, s, _ = proj.shape
    pairs = SB_HEADS * HEAD_DIM // LANES
    base = 3 * DIFF_HEADS * 2 * HEAD_DIM // LANES
    kern = functools.partial(_sb_attn_kernel, tq=tq, tiles=tiles)
    rows = tq * tiles
    return pl.pallas_call(
        kern,
        grid=(b, pairs, s // rows),
        in_specs=[
            pl.BlockSpec((1, rows, LANES), lambda bi, hi, qi: (bi, qi, base + hi)),
            pl.BlockSpec((1, s, LANES), lambda bi, hi, qi: (bi, 0, base + pairs + hi)),
            pl.BlockSpec((1, s, LANES), lambda bi, hi, qi: (bi, 0, base + 2 * pairs + hi)),
        ],
        out_specs=pl.BlockSpec((1, rows, LANES), lambda bi, hi, qi: (bi, qi, hi)),
        out_shape=jax.ShapeDtypeStruct((b, s, pairs * LANES), BF16),
        compiler_params=pltpu.CompilerParams(
            dimension_semantics=("arbitrary", "arbitrary", "arbitrary"), vmem_limit_bytes=VMEM_LIMIT),
        name="sb_attn",
    )(proj, proj, proj)


def _mix_ffn_kernel(*refs, n_y, sub, final, ffn_width, n_steps):
    h_ref = refs[0]
    y_refs = refs[1:1 + n_y]
    wo_refs = refs[1 + n_y:1 + 2 * n_y]
    fg_ref, wg_ref, wu_ref, wd_ref = refs[1 + 2 * n_y:5 + 2 * n_y]
    fin_ref = refs[5 + 2 * n_y] if final else None
    o_ref, n_ref, acc_ref = refs[-3:]
    j = pl.program_id(1)

    @pl.when(j == 0)
    def _():
        h1 = h_ref[...]
        for y_ref, wo_ref in zip(y_refs, wo_refs):
            h1 = h1 + _dot(y_ref[...], wo_ref[...])
        n_ref[...] = _rms(h1, fg_ref[...], NORM_EPS).astype(BF16)
        acc_ref[...] = h1

    n = n_ref[...]
    fc = wg_ref.shape[1]
    last_j = pl.num_programs(1) - 1

    def chunk(lo):
        g = _dot(n, wg_ref[:, lo:lo + sub])
        u = _dot(n, wu_ref[:, lo:lo + sub])
        return _dot((_silu(g) * u).astype(BF16), wd_ref[lo:lo + sub, :])

    chunks = list(range(0, fc, sub))
    beyond = [lo for lo in chunks if (n_steps - 1) * fc + lo + sub > ffn_width]
    for lo in beyond:
        @pl.when(j < last_j)
        def _(lo=lo):
            acc_ref[...] += chunk(lo)

    acc = acc_ref[...]
    for lo in chunks:
        if lo not in beyond:
            acc = acc + chunk(lo)
    acc_ref[...] = acc

    @pl.when(j == last_j)
    def _():
        out = acc_ref[...]
        if final:
            out = _rms(out, fin_ref[...], NORM_EPS)
        o_ref[...] = out


def _ffn_tiles(f, n_steps=2):
    sub = MXU_COLS
    assert f % sub == 0, f
    n_sub = f // sub
    return -(-n_sub // n_steps) * sub, sub


def _mix_ffn(h2d, ys, wos, fg, wg, wu, wd, layer, fin, tm, fc, sub):
    m, d = h2d.shape
    f = wg.shape[2]
    n_steps = -(-f // fc)
    final = fin is not None
    kern = functools.partial(_mix_ffn_kernel, n_y=len(ys), sub=sub, final=final, ffn_width=f, n_steps=n_steps)
    const = lambda shape: pl.BlockSpec(shape, lambda i, j: (0,) * len(shape))
    in_specs = [pl.BlockSpec((tm, d), lambda i, j: (i, 0))]
    in_specs += [pl.BlockSpec((tm, y.shape[1]), lambda i, j: (i, 0)) for y in ys]
    in_specs += [pl.BlockSpec((y.shape[1], d), lambda i, j, k=k: (k, 0)) for k, y in enumerate(ys)]
    in_specs += [const(fg.shape),
                 pl.BlockSpec((None, d, fc), lambda i, j: (layer, 0, j)),
                 pl.BlockSpec((None, d, fc), lambda i, j: (layer, 0, j)),
                 pl.BlockSpec((None, fc, d), lambda i, j: (layer, j, 0))]
    args = [h2d, *ys, *wos, fg, wg, wu, wd]
    if final:
        in_specs.append(const(fin.shape))
        args.append(fin)
    return pl.pallas_call(
        kern,
        grid=(m // tm, n_steps),
        in_specs=in_specs,
        out_specs=pl.BlockSpec((tm, d), lambda i, j: (i, 0)),
        out_shape=jax.ShapeDtypeStruct((m, d), F32),
        scratch_shapes=[pltpu.VMEM((tm, d), BF16), pltpu.VMEM((tm, d), F32)],
        compiler_params=pltpu.CompilerParams(
            dimension_semantics=("arbitrary", "arbitrary"), vmem_limit_bytes=VMEM_LIMIT),
        name="mix_ffn",
    )(*args)


def _ssd_kernel(zx_ref, dt_ref, dtb_ref, alog_ref, dsk_ref, gn_ref, exp_ref, o_ref, state, *, chunk):
    L = chunk
    gw = SSD_HEADS * SSD_HEAD_DIM // SSD_GROUPS

    @pl.when(pl.program_id(1) == 0)
    def _():
        state[...] = jnp.zeros(state.shape, F32)

    lane = lax.broadcasted_iota(jnp.int32, (1, LANES), 1)
    a = jnp.where(lane < SSD_HEADS, -jnp.exp(alog_ref[...]), 0.0)
    row = lax.broadcasted_iota(jnp.int32, (L, L), 0)
    col = lax.broadcasted_iota(jnp.int32, (L, L), 1)
    lower = row >= col
    tril = jnp.where(lower, 1.0, 0.0).astype(BF16)
    e = exp_ref[...]
    glane = lax.broadcasted_iota(jnp.int32, (1, gw), 1)
    for r0 in range(0, zx_ref.shape[1], L):
        _ssd_chunk(zx_ref, dt_ref, dtb_ref, dsk_ref, gn_ref, o_ref, state, r0, L, a, lower, tril, e, glane)


def _ssd_chunk(zx_ref, dt_ref, dtb_ref, dsk_ref, gn_ref, o_ref, state, r0, L, a, lower, tril, e, glane):
    d_inner = SSD_HEADS * SSD_HEAD_DIM
    gw = d_inner // SSD_GROUPS
    hpg = SSD_HEADS // SSD_GROUPS
    x_off, b_off, c_off = d_inner, 2 * d_inner, 2 * d_inner + SSD_GROUPS * SSD_STATE
    rows = slice(r0, r0 + L)
    dt = _softplus(dt_ref[0, rows, :] + dtb_ref[...])
    d_hi, d_mid, d_lo = _split3(dt * a)
    acum = _dot(tril, d_hi) + _dot(tril, d_mid) + _dot(tril, d_lo)
    last = acum[L - 1:L, :]
    ea = jnp.exp(acum)
    w = jnp.exp(last - acum) * dt
    acum_t = acum.T
    dt_t = dt.T
    w_t = w.T
    c_hi, c_mid, c_lo = _split3(jnp.broadcast_to(jnp.exp(last), (SUBLANES, LANES)))
    cd_full = (_dot(c_hi, e) + _dot(c_mid, e) + _dot(c_lo, e))[0:1, :]

    for g in range(SSD_GROUPS):
        xg_bf = zx_ref[0, rows, x_off + g * gw:x_off + (g + 1) * gw]
        xg = xg_bf.astype(F32)
        bg_bf = zx_ref[0, rows, b_off + g * SSD_STATE:b_off + (g + 1) * SSD_STATE]
        cg_bf = zx_ref[0, rows, c_off + g * SSD_STATE:c_off + (g + 1) * SSD_STATE]
        cg = cg_bf.astype(F32)
        cb = _dot_nt(cg_bf, bg_bf)
        bg_t = bg_bf.astype(F32).T
        prev = state[g]
        prev_bf = prev.astype(BF16)
        y_lhs, y_rhs, st_lhs, st_rhs = [], [], [], []
        for j in range(hpg):
            h = g * hpg + j
            in_head = jnp.logical_and(glane >= j * SSD_HEAD_DIM, glane < (j + 1) * SSD_HEAD_DIM)
            x_h = jnp.where(in_head, xg_bf, jnp.zeros_like(xg_bf))
            prev_h = jnp.where(in_head, prev_bf, jnp.zeros_like(prev_bf))
            seg = acum[:, h:h + 1] - acum_t[h:h + 1, :]
            decay = jnp.exp(jnp.where(lower, seg, -jnp.inf))
            y_lhs += [(cb * decay * dt_t[h:h + 1, :]).astype(BF16), (cg * ea[:, h:h + 1]).astype(BF16)]
            y_rhs += [x_h, prev_h]
            st_lhs.append((bg_t * w_t[h:h + 1, :]).astype(BF16))
            st_rhs.append(x_h)
        y = dsk_ref[:, g * gw:(g + 1) * gw] * xg + _dot(jnp.concatenate(y_lhs, axis=1), jnp.concatenate(y_rhs, axis=0))
        state[g] = prev * cd_full[:, g * gw:(g + 1) * gw] + _dot(
            jnp.concatenate(st_lhs, axis=1), jnp.concatenate(st_rhs, axis=0))
        gated = y * zx_ref[0, rows, g * gw:(g + 1) * gw].astype(F32)
        o_ref[0, rows, g * gw:(g + 1) * gw] = _rms(
            gated, gn_ref[:, g * gw:(g + 1) * gw], SUBLN_EPS).astype(o_ref.dtype)


def _ssd_scan(zx, dt_raw, dt_bias, a_log, d_full, gnorm, expand, chunk, chunks_per_step):
    b, s, width = zx.shape
    d_inner = SSD_HEADS * SSD_HEAD_DIM
    rows = chunk * chunks_per_step
    kern = functools.partial(_ssd_kernel, chunk=chunk)
    const = lambda shape: pl.BlockSpec(shape, lambda bi, ci: (0,) * len(shape))
    return pl.pallas_call(
        kern,
        grid=(b, s // rows),
        in_specs=[
            pl.BlockSpec((1, rows, width), lambda bi, ci: (bi, ci, 0)),
            pl.BlockSpec((1, rows, LANES), lambda bi, ci: (bi, ci, 0)),
            const(dt_bias.shape), const(a_log.shape),
            const(d_full.shape), const(gnorm.shape), const(expand.shape),
        ],
        out_specs=pl.BlockSpec((1, rows, d_inner), lambda bi, ci: (bi, ci, 0)),
        out_shape=jax.ShapeDtypeStruct((b, s, d_inner), BF16),
        scratch_shapes=[pltpu.VMEM((SSD_GROUPS, SSD_STATE, d_inner // SSD_GROUPS), F32)],
        compiler_params=pltpu.CompilerParams(
            dimension_semantics=("arbitrary", "arbitrary"), vmem_limit_bytes=VMEM_LIMIT),
        name="ssd_scan",
    )(zx, dt_raw, dt_bias, a_log, d_full, gnorm, expand)


def _rope_tables(seq):
    half = HEAD_DIM // 2
    inv_freq = ROPE_THETA ** (-jnp.arange(half, dtype=F32) / half)
    ang = jnp.arange(seq, dtype=F32)[:, None] * inv_freq[None, :]
    cos, sin = jnp.cos(ang), jnp.sin(ang)
    reps = LANES // HEAD_DIM
    return jnp.tile(jnp.concatenate([cos, cos], axis=1), (1, reps)), jnp.tile(jnp.concatenate([-sin, sin], axis=1), (1, reps))


def _pad_lanes(v):
    return jnp.pad(v, (0, LANES - v.shape[0]))[None, :]


def _attn_layer(h2d, b, s, norm_w, w_in, lq1, lk1, lq2, lk2, subln, w_out, ffn, lambda_init,
                tm, ffn_tm, tq_diff, tq_sb, sb_tiles):
    cos, sin = _rope_tables(s)
    proj = _attn_proj(h2d, norm_w[None, :], w_in.astype(BF16), cos, sin, s, ffn_tm)
    proj = proj.reshape(b, s, -1)
    lqk = jnp.stack([lq1, lk1, lq2, lk2])
    o_diff = _diff_attn(proj, lqk, subln[None, :], lambda_init, tq_diff)
    o_sb = _sb_attn(proj, tq_sb, sb_tiles)
    assert o_diff.shape[-1] == o_sb.shape[-1]
    w_out_bf = w_out.astype(BF16)
    return _mix_ffn(h2d, [o_diff.reshape(b * s, -1), o_sb.reshape(b * s, -1)], [w_out_bf, w_out_bf], *ffn, ffn_tm,
                    *_ffn_tiles(ffn[1].shape[2]))


def _ssd_layer(h2d, b, s, norm_w, w_in, conv_w, conv_b, dt_bias, a_log, d_skip, gnorm, w_out, ffn, tm, ffn_tm, chunk):
    d_inner = SSD_HEADS * SSD_HEAD_DIM
    main = w_in.shape[1] - SSD_HEADS
    w_bf = w_in.astype(BF16)
    w_dt = jnp.pad(w_bf[:, main:], ((0, 0), (0, LANES - SSD_HEADS)))
    zx, dt_raw = _ssd_proj(h2d, norm_w[None, :], w_bf, w_dt, conv_w, conv_b[None, :], s, tm)
    expand = (jnp.arange(LANES)[:, None] == (jnp.arange(d_inner)[None, :] // SSD_HEAD_DIM)).astype(BF16)
    y = _ssd_scan(zx.reshape(b, s, main), dt_raw.reshape(b, s, LANES),
                  _pad_lanes(dt_bias), _pad_lanes(a_log), jnp.repeat(d_skip, SSD_HEAD_DIM)[None, :],
                  gnorm[None, :], expand, chunk, 4 if s % (4 * chunk) == 0 else 1)
    return _mix_ffn(h2d, [y.reshape(b * s, d_inner)], [w_out.astype(BF16)], *ffn, ffn_tm,
                    *_ffn_tiles(ffn[1].shape[2], 3))


def kernel(x, attn_norm, attn_w_in, diff_lq1, diff_lk1, diff_lq2, diff_lk2, diff_subln, attn_w_out, ssd_norm, ssd_w_in, ssd_conv_w, ssd_conv_b, ssd_dt_bias, ssd_a_log, ssd_d, ssd_gnorm, ssd_w_out, ffn_norm, ffn_w_gate, ffn_w_up, ffn_w_down, final_norm):
    b, s, d = x.shape
    depth = ffn_norm.shape[0]
    tm = min(512, s)
    h = x.reshape(b * s, d)
    wg, wu, wd = ffn_w_gate.astype(BF16), ffn_w_up.astype(BF16), ffn_w_down.astype(BF16)
    for layer in range(depth):
        i = layer // 2
        fin = final_norm[None, :] if layer == depth - 1 else None
        ffn = (ffn_norm[layer][None, :], wg, wu, wd, layer, fin)
        if layer % 2 == 0:
            lambda_init = 0.8 - 0.6 * math.exp(-0.3 * layer)
            h = _attn_layer(h, b, s, attn_norm[i], attn_w_in[i], diff_lq1[i], diff_lk1[i], diff_lq2[i],
                            diff_lk2[i], diff_subln[i], attn_w_out[i], ffn, lambda_init,
                            tm=tm, ffn_tm=min(1024, s), tq_diff=min(512, s), tq_sb=min(256, s), sb_tiles=4 if s % 1024 == 0 else 1)
        else:
            h = _ssd_layer(h, b, s, ssd_norm[i], ssd_w_in[i], ssd_conv_w[i], ssd_conv_b[i], ssd_dt_bias[i],
                           ssd_a_log[i], ssd_d[i], ssd_gnorm[i], ssd_w_out[i], ffn,
                           tm=tm, ffn_tm=min(1024, s), chunk=128)
    return h.reshape(b, s, d)
```
